```python
import jax, jax.numpy as jnp
from jax import lax
import numpy as np

D_MODEL = 2048
BATCH = 2
SEQ = 8192
DEPTH = 1

CHUNK = 64
D_BRANCH = D_MODEL // 2
HEAD_DIM = 64
RWKV_HEADS = D_BRANCH // HEAD_DIM
ATTN_HEADS = D_BRANCH // HEAD_DIM
LORA_DECAY = 64
LORA_ICLR = 64
DECAY_SCALE = 0.606531
PAST_CHUNKS = 8
PAST = PAST_CHUNKS * CHUNK
BAND = (PAST_CHUNKS + 1) * CHUNK
REL_CLIP = 256
N_BRANCH = 2
NORM_EPS = 1e-6
GN_EPS = 64e-5
IN_SIZES = (D_BRANCH,) * 8 + (D_MODEL,) * 2
D_IN = sum(IN_SIZES)

kernel_name = "hybrid_rwkv7_chunkattn_gated_block"


def _rms_norm(x, g):
    xf = x.astype(jnp.float32)
    y = xf * lax.rsqrt(jnp.mean(xf * xf, axis=-1, keepdims=True) + NORM_EPS)
    return (y * g.astype(jnp.float32)).astype(x.dtype)


def _shift(x):
    return jnp.pad(x, ((0, 0), (1, 0), (0, 0)))[:, :-1]


def _rwkv7_mixer(h, p_r, p_k, p_v, mu_rkv, mu_wa, w0, w1, w2, a0, a1, a2,
                 k_k, k_a, r_k, ln_x_g, ln_x_b):
    B, S, _ = h.shape
    H, N = RWKV_HEADS, HEAD_DIM
    f32 = jnp.float32
    r = p_r + (_shift(p_r) - p_r) * mu_rkv[0]
    k = p_k + (_shift(p_k) - p_k) * mu_rkv[1]
    v = p_v + (_shift(p_v) - p_v) * mu_rkv[2]
    dh = _shift(h) - h
    xw = h + dh * mu_wa[0]
    xa = h + dh * mu_wa[1]
    w = jnp.exp(-DECAY_SCALE * jax.nn.sigmoid((w0 + jnp.tanh(xw @ w1) @ w2).astype(f32)))
    a = jax.nn.sigmoid((a0 + (xa @ a1) @ a2).astype(f32))
    r = r.astype(f32)
    k = k.astype(f32)
    v = v.astype(f32)
    kk = (k * k_k.astype(f32)).reshape(B, S, H, N)
    kk = kk / jnp.maximum(jnp.sqrt(jnp.sum(kk * kk, axis=-1, keepdims=True)), 1e-12)
    k = k * (1.0 + (a - 1.0) * k_a.astype(f32))
    r4, k4, v4 = (t.reshape(B, S, H, N) for t in (r, k, v))
    w4, a4 = w.reshape(B, S, H, N), a.reshape(B, S, H, N)

    def step(state, inp):
        rt, wt, kt, vt, kkt, at = inp
        sa = jnp.einsum('bhvk,bhk->bhv', state, -kkt)
        state = (state * wt[:, :, None, :] + sa[..., None] * (kkt * at)[:, :, None, :]
                 + vt[..., :, None] * kt[..., None, :])
        yt = jnp.einsum('bhvk,bhk->bhv', state, rt)
        return state, yt

    seq_major = tuple(jnp.swapaxes(t, 0, 1) for t in (r4, w4, k4, v4, kk, a4))
    state0 = jnp.zeros((B, H, N, N), f32)
    _, y = lax.scan(step, state0, seq_major)
    y = jnp.swapaxes(y, 0, 1)
    mean = jnp.mean(y, axis=-1, keepdims=True)
    var = jnp.mean(jnp.square(y - mean), axis=-1, keepdims=True)
    y = ((y - mean) * lax.rsqrt(var + GN_EPS)).reshape(B, S, D_BRANCH)
    y = y * ln_x_g.astype(f32) + ln_x_b.astype(f32)
    bonus = jnp.sum(r4 * k4 * r_k.astype(f32), axis=-1, keepdims=True) * v4
    y = y + bonus.reshape(B, S, D_BRANCH)
    return y.astype(h.dtype)


def _chunk_band_attention(q, k, v, rel_bias):
    B, S, _ = q.shape
    H, Dh = ATTN_HEADS, HEAD_DIM
    n_chunks = S // CHUNK
    q = q.reshape(B, S, H, Dh) * (Dh ** -0.5)
    kp = jnp.pad(k.reshape(B, S, H, Dh), ((0, 0), (PAST, 0), (0, 0), (0, 0)))
    vp = jnp.pad(v.reshape(B, S, H, Dh), ((0, 0), (PAST, 0), (0, 0), (0, 0)))
    qi = jnp.arange(CHUNK)[:, None]
    kj = jnp.arange(BAND)[None, :]
    rel_idx = jnp.clip(qi - kj + PAST, -REL_CLIP, REL_CLIP) + REL_CLIP
    bias = rel_bias[:, rel_idx].astype(jnp.float32)
    key_offsets = jnp.arange(BAND) - PAST

    def one_chunk(c):
        start = c * CHUNK
        qc = lax.dynamic_slice_in_dim(q, start, CHUNK, axis=1)
        kc = lax.dynamic_slice_in_dim(kp, start, BAND, axis=1)
        vc = lax.dynamic_slice_in_dim(vp, start, BAND, axis=1)
        s = jnp.einsum('bqhd,bkhd->bhqk', qc, kc).astype(jnp.float32) + bias
        valid = (start + key_offsets) >= 0
        s = jnp.where(valid[None, None, None, :], s, jnp.float32(-1e30))
        p = jax.nn.softmax(s, axis=-1).astype(vc.dtype)
        return jnp.einsum('bhqk,bkhd->bqhd', p, vc)

    o = lax.map(one_chunk, jnp.arange(n_chunks))
    return jnp.moveaxis(o, 0, 1).reshape(B, S, H * Dh)


def setup_inputs(seed: int = 0) -> dict:
    key = jax.random.key(seed)
    ks = jax.random.split(key, 24)
    L, D, DB, H, N = DEPTH, D_MODEL, D_BRANCH, RWKV_HEADS, HEAD_DIM
    nrm = lambda k, shape, s: jax.random.normal(k, shape, jnp.float32) * s
    return {
        "x": nrm(ks[0], (BATCH, SEQ, D), 1.0),
        "pre_norm_g": 1.0 + nrm(ks[1], (L, D), 0.02),
        "post_norm_g": 1.0 + nrm(ks[2], (L, D), 0.02),
        "w_in": nrm(ks[3], (L, D, D_IN), D ** -0.5),
        "mu_rkv": jax.random.uniform(ks[4], (L, 3, DB), jnp.float32),
        "mu_wa": jax.random.uniform(ks[5], (L, 2, D), jnp.float32),
        "w0": -1.5 + nrm(ks[6], (L, DB), 1.0),
        "w1": nrm(ks[7], (L, D, LORA_DECAY), D ** -0.5),
        "w2": nrm(ks[8], (L, LORA_DECAY, DB), 0.5 * LORA_DECAY ** -0.5),
        "a0": nrm(ks[9], (L, DB), 0.3),
        "a1": nrm(ks[10], (L, D, LORA_ICLR), D ** -0.5),
        "a2": nrm(ks[11], (L, LORA_ICLR, DB), 0.5 * LORA_ICLR ** -0.5),
        "k_k": 0.85 + nrm(ks[12], (L, DB), 0.02),
        "k_a": 1.0 + nrm(ks[13], (L, DB), 0.02),
        "r_k": nrm(ks[14], (L, H, N), 0.1),
        "ln_x_g": 1.0 + nrm(ks[15], (L, DB), 0.02),
        "ln_x_b": nrm(ks[16], (L, DB), 0.02),
        "rel_bias": nrm(ks[17], (L, ATTN_HEADS, 2 * REL_CLIP + 1), 0.1),
        "w_branch_rwkv": nrm(ks[18], (L, DB, D), DB ** -0.5),
        "w_branch_attn": nrm(ks[19], (L, DB, D), DB ** -0.5),
        "b_merge": nrm(ks[20], (L, N_BRANCH, D), 0.1),
        "w_out": nrm(ks[21], (L, D, D), D ** -0.5),
    }


def reference(x, pre_norm_g, post_norm_g, w_in, mu_rkv, mu_wa, w0, w1, w2, a0, a1, a2,
              k_k, k_a, r_k, ln_x_g, ln_x_b, rel_bias, w_branch_rwkv, w_branch_attn,
              b_merge, w_out):
    split_points = np.cumsum(IN_SIZES)[:-1].tolist()
    for l in range(DEPTH):
        h = _rms_norm(x, pre_norm_g[l])
        proj = h @ w_in[l]
        p_r, p_k, p_v, z_r, q_a, k_a_att, v_a, z_a, m_r, m_a = jnp.split(
            proj, split_points, axis=-1)
        y_r = _rwkv7_mixer(h, p_r, p_k, p_v, mu_rkv[l], mu_wa[l], w0[l], w1[l], w2[l],
                           a0[l], a1[l], a2[l], k_k[l], k_a[l], r_k[l],
                           ln_x_g[l], ln_x_b[l]) * jax.nn.silu(z_r)
        y_a = _chunk_band_attention(q_a, k_a_att, v_a, rel_bias[l]) * jax.nn.silu(z_a)
        u_r = y_r @ w_branch_rwkv[l]
        u_a = y_a @ w_branch_attn[l]
        merged = (jax.nn.sigmoid(m_r + b_merge[l, 0]) * u_r
                  + jax.nn.sigmoid(m_a + b_merge[l, 1]) * u_a)
        o = merged @ w_out[l]
        x = x + _rms_norm(o, post_norm_g[l])
    return x
```

```python
import functools

import jax
import jax.numpy as jnp
from jax import lax
from jax.experimental import pallas as pl
from jax.experimental.pallas import tpu as pltpu

F32 = jnp.float32
BF16 = jnp.bfloat16

D_MODEL = 2048
D_BRANCH = 1024
HEAD_DIM = 64
N_HEADS = 16
N_PAIRS = N_HEADS // 2
LANES = 128
CHUNK = 64
LORA = 64
DECAY_SCALE = 0.606531
PAST_CHUNKS = 8
REL_CLIP = 256
NORM_EPS = 1e-6
GN_EPS = 64e-5
NEG_INF = -1e30
D_IN = 8 * D_BRANCH + 2 * D_MODEL
D_IN_AUG = D_IN + 4 * LORA
VMEM_LIMIT = 56 * 1024 * 1024

ATT_QCH = 2
ATT_Q = ATT_QCH * CHUNK
ATT_KCH = PAST_CHUNKS + ATT_QCH
ATT_K = ATT_KCH * CHUNK


def _inproj_kernel(x_ref, g_ref, w_ref, o_ref, hn_ref):
    @pl.when(pl.program_id(1) == 0)
    def _():
        x = x_ref[...]
        ms = jnp.mean(x * x, axis=-1, keepdims=True)
        hn_ref[...] = (x * lax.rsqrt(ms + NORM_EPS) * g_ref[...]).astype(BF16)

    o_ref[...] = jnp.dot(hn_ref[...], w_ref[...],
                         preferred_element_type=F32).astype(BF16)


def _inproj(x2, g, w_aug, tm=512, tn=1792):
    m = x2.shape[0]
    n = w_aug.shape[1]
    return pl.pallas_call(
        _inproj_kernel,
        grid=(m // tm, n // tn),
        in_specs=[
            pl.BlockSpec((tm, D_MODEL), lambda i, j: (i, 0)),
            pl.BlockSpec((1, D_MODEL), lambda i, j: (0, 0)),
            pl.BlockSpec((D_MODEL, tn), lambda i, j: (0, j)),
        ],
        out_specs=pl.BlockSpec((tm, tn), lambda i, j: (i, j)),
        out_shape=jax.ShapeDtypeStruct((m, n), BF16),
        scratch_shapes=[pltpu.VMEM((tm, D_MODEL), BF16)],
        compiler_params=pltpu.CompilerParams(
            dimension_semantics=("parallel", "arbitrary"),
            vmem_limit_bytes=VMEM_LIMIT),
        name="inproj",
    )(x2, g, w_aug)


def _block_diag(x, lo_half):
    zero = jnp.zeros_like(x)
    return jnp.concatenate(
        [jnp.where(lo_half, x, zero), jnp.where(lo_half, zero, x)], axis=0)


def _rwkv_kernel(pr_ref, pk_ref, pv_ref, zr_ref, lo_ref,
                 mu_ref, w0_ref, a0_ref, kk_ref, ka_ref, rk_ref, lng_ref, lnb_ref,
                 w2a_ref, ones_ref, tri_ref,
                 o_ref,
                 br_ref, bk_ref, bv_ref, bl_ref, st_ref):
    c = pl.program_id(1)
    L = CHUNK

    @pl.when(c == 0)
    def _():
        br_ref[...] = jnp.zeros_like(br_ref)
        bk_ref[...] = jnp.zeros_like(bk_ref)
        bv_ref[...] = jnp.zeros_like(bv_ref)
        bl_ref[...] = jnp.zeros_like(bl_ref)
        st_ref[...] = jnp.zeros_like(st_ref)

    def shifted(buf_ref, cur):
        buf_ref[8:8 + L, :] = cur
        prev = buf_ref[7:7 + L, :]
        buf_ref[7:8, :] = cur[L - 1:L, :]
        return prev

    pr = pr_ref[...].astype(F32)
    pk = pk_ref[...].astype(F32)
    pv = pv_ref[...].astype(F32)
    r = pr + (shifted(br_ref, pr) - pr) * mu_ref[0:1, :]
    k = pk + (shifted(bk_ref, pk) - pk) * mu_ref[1:2, :]
    v = pv + (shifted(bv_ref, pv) - pv) * mu_ref[2:3, :]

    lo = lo_ref[...].astype(F32)
    hid = lo[:, :LANES] + shifted(bl_ref, lo[:, LANES:])
    lane1 = lax.broadcasted_iota(jnp.int32, (L, LANES), 1)
    hid = jnp.where(lane1 < LORA, jnp.tanh(hid), hid)
    dec = jnp.dot(hid.astype(BF16), w2a_ref[...], preferred_element_type=F32)
    lw = -DECAY_SCALE * jax.nn.sigmoid(w0_ref[...] + dec[:, :D_BRANCH])
    a = jax.nn.sigmoid(a0_ref[...] + dec[:, D_BRANCH:])

    ones_bd = ones_ref[...]

    def head_sum(x):
        xb = x.astype(BF16)
        return jnp.concatenate(
            [jnp.dot(xb[:, p * LANES:(p + 1) * LANES], ones_bd,
                     preferred_element_type=F32) for p in range(N_PAIRS)], axis=1)

    kk = k * kk_ref[...]
    nrm = jnp.sqrt(head_sum(kk * kk))
    kk = kk / jnp.maximum(nrm, 1e-12)
    k = k * (1.0 + (a - 1.0) * ka_ref[...])
    ap = -kk
    bp = kk * a

    tri = tri_ref[...]
    lw_hi = lw.astype(BF16)
    lw_lo = (lw - lw_hi.astype(F32)).astype(BF16)
    cum = (jnp.dot(tri, lw_hi, preferred_element_type=F32)
           + jnp.dot(tri, lw_lo, preferred_element_type=F32))
    cum_l = cum[L - 1:L, :]
    e_pos = jnp.exp(cum)
    e_neg = jnp.exp(-cum)
    e_end = jnp.exp(cum_l - cum)
    e_prev = jnp.exp(cum - lw)
    p_l = jnp.exp(cum_l)

    at_all = (ap * e_prev).astype(BF16)
    rt_all = (r * e_pos).astype(BF16)
    bt_all = (bp * e_neg).astype(BF16)
    kt_all = (k * e_neg).astype(BF16)
    bh_all = bp * e_end
    kh_all = k * e_end
    v_bf = v.astype(BF16)

    row = lax.broadcasted_iota(jnp.int32, (L, LANES), 0)
    li = lane1 & (HEAD_DIM - 1)
    strict = row > li
    incl = row >= li
    lo_half = lane1 < HEAD_DIM
    eye_p = (row == li).astype(F32)
    nt = (((1,), (1,)), ((), ()))

    ys = []
    for p in range(N_PAIRS):
        sl = slice(p * LANES, (p + 1) * LANES)
        at, rt, bt, kt, vp = at_all[:, sl], rt_all[:, sl], bt_all[:, sl], kt_all[:, sl], v_bf[:, sl]
        st = st_ref[p]

        lhs = jnp.concatenate([at, rt], axis=0)
        rhs = jnp.concatenate([_block_diag(bt, lo_half), _block_diag(kt, lo_half)], axis=0)
        aa = lax.dot_general(lhs, rhs, nt, preferred_element_type=F32)
        a_ab = jnp.where(strict, aa[:L, :LANES], 0.0)
        a_ak = jnp.where(strict, aa[:L, LANES:], 0.0)
        a_rb = jnp.where(incl, aa[L:, :LANES], 0.0)
        a_rk = jnp.where(incl, aa[L:, LANES:], 0.0)

        nm = a_ab
        tp = eye_p + nm
        for _ in range(5):
            nb = nm.astype(BF16)
            nm = jnp.dot(nb, _block_diag(nb, lo_half), preferred_element_type=F32)
            tp = tp + jnp.dot(nm.astype(BF16), _block_diag(tp.astype(BF16), lo_half),
                              preferred_element_type=F32)

        st_bd = _block_diag(st.astype(BF16), lo_half)
        v_bd = _block_diag(vp, lo_half)
        x = jnp.dot(jnp.concatenate([at, a_ak.astype(BF16)], axis=1),
                    jnp.concatenate([st_bd, v_bd], axis=0), preferred_element_type=F32)
        u = jnp.dot(tp.astype(BF16), _block_diag(x.astype(BF16), lo_half),
                    preferred_element_type=F32)
        u_bf = u.astype(BF16)
        y = jnp.dot(jnp.concatenate([rt, a_rb.astype(BF16), a_rk.astype(BF16)], axis=1),
                    jnp.concatenate([st_bd, _block_diag(u_bf, lo_half), v_bd], axis=0),
                    preferred_element_type=F32)
        ys.append(y)

        bkh_t = jnp.concatenate([bh_all[:, sl], kh_all[:, sl]], axis=0).T.astype(BF16)
        s_full = jnp.dot(bkh_t, jnp.concatenate([u_bf, vp], axis=0),
                         preferred_element_type=F32)
        dcol = jnp.broadcast_to(p_l[:, sl], (LANES, LANES)).T
        dp = jnp.where(lo_half, dcol[:L], dcol[L:])
        st_ref[p] = dp * st + jnp.where(lo_half, s_full[:L], s_full[L:])

    y = jnp.concatenate(ys, axis=1)

    inv_n = 1.0 / HEAD_DIM
    mean = head_sum(y) * inv_n
    yc = y - mean
    var = head_sum(yc * yc) * inv_n
    yn = yc * lax.rsqrt(var + GN_EPS) * lng_ref[...] + lnb_ref[...]
    bonus = head_sum(r * k * rk_ref[...]) * v
    z = zr_ref[...].astype(F32)
    o_ref[...] = ((yn + bonus) * (z * jax.nn.sigmoid(z))).astype(BF16)


def _rwkv(proj, b, s, mu_rkv, w0, a0, k_k, k_a, r_k, ln_g, ln_b, w2a, ones_bd, tri):
    nc = s // CHUNK
    row = lambda bi, ci: bi * nc + ci
    col = lambda j: pl.BlockSpec((CHUNK, D_BRANCH), lambda bi, ci: (row(bi, ci), j))
    const = lambda shape: pl.BlockSpec(shape, lambda bi, ci: (0,) * len(shape))
    return pl.pallas_call(
        _rwkv_kernel,
        grid=(b, nc),
        in_specs=[
            col(0), col(1), col(2), col(3),
            pl.BlockSpec((CHUNK, 4 * LORA), lambda bi, ci: (row(bi, ci), D_IN // (4 * LORA))),
            const((3, D_BRANCH)), const((1, D_BRANCH)), const((1, D_BRANCH)),
            const((1, D_BRANCH)), const((1, D_BRANCH)), const((1, D_BRANCH)),
            const((1, D_BRANCH)), const((1, D_BRANCH)),
            const((2 * LORA, 2 * D_BRANCH)), const((LANES, LANES)), const((CHUNK, CHUNK)),
        ],
        out_specs=pl.BlockSpec((CHUNK, D_BRANCH), lambda bi, ci: (row(bi, ci), 0)),
        out_shape=jax.ShapeDtypeStruct((b * s, D_BRANCH), BF16),
        scratch_shapes=[
            pltpu.VMEM((8 + CHUNK, D_BRANCH), F32),
            pltpu.VMEM((8 + CHUNK, D_BRANCH), F32),
            pltpu.VMEM((8 + CHUNK, D_BRANCH), F32),
            pltpu.VMEM((8 + CHUNK, LANES), F32),
            pltpu.VMEM((N_PAIRS, HEAD_DIM, LANES), F32),
        ],
        compiler_params=pltpu.CompilerParams(
            dimension_semantics=("arbitrary", "arbitrary"),
            vmem_limit_bytes=VMEM_LIMIT),
        name="rwkv7",
    )(proj, proj, proj, proj, proj, mu_rkv, w0, a0, k_k, k_a, r_k, ln_g, ln_b, w2a, ones_bd, tri)


def _attn_kernel(*refs):
    q_ref = refs[0]
    k_refs = refs[1:1 + ATT_KCH]
    v_refs = refs[1 + ATT_KCH:1 + 2 * ATT_KCH]
    z_ref, bias_ref, o_ref = refs[1 + 2 * ATT_KCH:]
    i = pl.program_id(1)

    q = q_ref[...] * jnp.asarray(HEAD_DIM ** -0.5, BF16)
    kcat = jnp.concatenate([kr[...] for kr in k_refs], axis=0)
    vcat = jnp.concatenate([vr[...] for vr in v_refs], axis=0)

    klane = lax.broadcasted_iota(jnp.int32, (1, ATT_K), 1)
    kchunk = ATT_QCH * i - PAST_CHUNKS + klane // CHUNK
    start_mask = jnp.where(kchunk >= 0, 0.0, NEG_INF).astype(F32)

    lane = lax.broadcasted_iota(jnp.int32, (ATT_Q, LANES), 1)
    lo_half = lane < HEAD_DIM
    nt = (((1,), (1,)), ((), ()))
    outs = []
    for p in range(N_PAIRS):
        sl = slice(p * LANES, (p + 1) * LANES)
        qp, kp, vp = q[:, sl], kcat[:, sl], vcat[:, sl]
        halves = []
        for hh in range(2):
            qm = jnp.where(lo_half if hh == 0 else jnp.logical_not(lo_half), qp,
                           jnp.zeros_like(qp))
            s = lax.dot_general(qm, kp, nt, preferred_element_type=F32)
            s = s + bias_ref[2 * p + hh] + start_mask
            m = jnp.max(s, axis=-1, keepdims=True)
            e = jnp.exp(s - m)
            l = jnp.sum(e, axis=-1, keepdims=True)
            oh = jnp.dot(e.astype(BF16), vp, preferred_element_type=F32)
            halves.append(oh / l)
        outs.append(jnp.where(lo_half, halves[0], halves[1]))
    o = jnp.concatenate(outs, axis=1)
    z = z_ref[...].astype(F32)
    o_ref[...] = (o * (z * jax.nn.sigmoid(z))).astype(BF16)


def _attention(proj, b, s, bias):
    nq = s // ATT_Q
    nc = s // CHUNK
    qrow = lambda bi, qi: bi * nq + qi

    def kv_spec(colblk, j):
        def imap(bi, qi):
            return (bi * nc + jnp.maximum(ATT_QCH * qi - PAST_CHUNKS + j, 0), colblk)
        return pl.BlockSpec((CHUNK, D_BRANCH), imap)

    in_specs = ([pl.BlockSpec((ATT_Q, D_BRANCH), lambda bi, qi: (qrow(bi, qi), 4))]
                + [kv_spec(5, j) for j in range(ATT_KCH)]
                + [kv_spec(6, j) for j in range(ATT_KCH)]
                + [pl.BlockSpec((ATT_Q, D_BRANCH), lambda bi, qi: (qrow(bi, qi), 7)),
                   pl.BlockSpec((N_HEADS, ATT_Q, ATT_K), lambda bi, qi: (0, 0, 0))])
    return pl.pallas_call(
        _attn_kernel,
        grid=(b, nq),
        in_specs=in_specs,
        out_specs=pl.BlockSpec((ATT_Q, D_BRANCH), lambda bi, qi: (qrow(bi, qi), 0)),
        out_shape=jax.ShapeDtypeStruct((b * s, D_BRANCH), BF16),
        compiler_params=pltpu.CompilerParams(
            dimension_semantics=("parallel", "arbitrary"),
            vmem_limit_bytes=VMEM_LIMIT),
        name="band_attn",
    )(*([proj] * (2 + 2 * ATT_KCH)), bias)


def _attn_bias_table(rel_bias):
    qi = jnp.arange(ATT_Q)[:, None]
    kj = jnp.arange(ATT_K)[None, :]
    rel = jnp.clip(qi - kj + PAST_CHUNKS * CHUNK, -REL_CLIP, REL_CLIP) + REL_CLIP
    qc = qi // CHUNK
    kc = kj // CHUNK - PAST_CHUNKS
    valid = (kc <= qc) & (kc >= qc - PAST_CHUNKS)
    return jnp.where(valid[None], rel_bias[:, rel].astype(F32), NEG_INF)


def _out_kernel(x_ref, yr_ref, ya_ref, mr_ref, ma_ref, wbr_ref, wba_ref, wout_ref,
                bm_ref, pg_ref, o_ref):
    u_r = jnp.dot(yr_ref[...], wbr_ref[...], preferred_element_type=F32)
    u_a = jnp.dot(ya_ref[...], wba_ref[...], preferred_element_type=F32)
    g_r = jax.nn.sigmoid(mr_ref[...].astype(F32) + bm_ref[0:1, :])
    g_a = jax.nn.sigmoid(ma_ref[...].astype(F32) + bm_ref[1:2, :])
    merged = (g_r * u_r + g_a * u_a).astype(BF16)
    o = jnp.dot(merged, wout_ref[...], preferred_element_type=F32)
    ms = jnp.mean(o * o, axis=-1, keepdims=True)
    o_ref[...] = x_ref[...] + o * lax.rsqrt(ms + NORM_EPS) * pg_ref[...]


def _outproj(x2, y_r, y_a, proj, w_br, w_ba, w_out, b_merge, post_g, tm=256):
    m = x2.shape[0]
    const = lambda shape: pl.BlockSpec(shape, lambda i: (0,) * len(shape),
                                       pipeline_mode=pl.Buffered(1))
    return pl.pallas_call(
        _out_kernel,
        grid=(m // tm,),
        in_specs=[
            pl.BlockSpec((tm, D_MODEL), lambda i: (i, 0)),
            pl.BlockSpec((tm, D_BRANCH), lambda i: (i, 0)),
            pl.BlockSpec((tm, D_BRANCH), lambda i: (i, 0)),
            pl.BlockSpec((tm, D_MODEL), lambda i: (i, 4)),
            pl.BlockSpec((tm, D_MODEL), lambda i: (i, 5)),
            const((D_BRANCH, D_MODEL)), const((D_BRANCH, D_MODEL)),
            const((D_MODEL, D_MODEL)), const((2, D_MODEL)), const((1, D_MODEL)),
        ],
        out_specs=pl.BlockSpec((tm, D_MODEL), lambda i: (i, 0)),
        out_shape=jax.ShapeDtypeStruct((m, D_MODEL), F32),
        compiler_params=pltpu.CompilerParams(
            dimension_semantics=("parallel",),
            vmem_limit_bytes=VMEM_LIMIT),
        name="merge_outproj",
    )(x2, y_r, y_a, proj, proj, w_br, w_ba, w_out, b_merge, post_g)


def _layer(x2, b, s, pre_g, post_g, w_in, mu_rkv, mu_wa, w0, w1, w2, a0, a1, a2,
           k_k, k_a, r_k, ln_g, ln_b, rel_bias, w_br, w_ba, b_merge, w_out):
    mw, ma = mu_wa[0][:, None], mu_wa[1][:, None]
    w_aug = jnp.concatenate(
        [w_in, (1.0 - mw) * w1, (1.0 - ma) * a1, mw * w1, ma * a1], axis=1).astype(BF16)
    zeros = jnp.zeros((LORA, D_BRANCH), F32)
    w2a = jnp.concatenate([jnp.concatenate([w2, zeros], axis=1),
                           jnp.concatenate([zeros, a2], axis=1)], axis=0).astype(BF16)
    lane = jnp.arange(LANES)
    ones_bd = (lane[:, None] // HEAD_DIM == lane[None, :] // HEAD_DIM).astype(BF16)
    tri = jnp.tril(jnp.ones((CHUNK, CHUNK), BF16))
    row = lambda t: t.reshape(1, -1).astype(F32)

    proj = _inproj(x2, row(pre_g), w_aug)
    y_r = _rwkv(proj, b, s, mu_rkv.astype(F32), row(w0), row(a0), row(k_k), row(k_a),
                row(r_k), row(ln_g), row(ln_b), w2a, ones_bd, tri)
    y_a = _attention(proj, b, s, _attn_bias_table(rel_bias))
    return _outproj(x2, y_r, y_a, proj, w_br.astype(BF16), w_ba.astype(BF16),
                    w_out.astype(BF16), b_merge.astype(F32), row(post_g))


def kernel(x, pre_norm_g, post_norm_g, w_in, mu_rkv, mu_wa, w0, w1, w2, a0, a1, a2,
           k_k, k_a, r_k, ln_x_g, ln_x_b, rel_bias, w_branch_rwkv, w_branch_attn,
           b_merge, w_out):
    b, s, d = x.shape
    assert d == D_MODEL and s % ATT_Q == 0 and (b * s) % 512 == 0
    x2 = x.reshape(b * s, d)
    for l in range(pre_norm_g.shape[0]):
        x2 = _layer(x2, b, s, pre_norm_g[l], post_norm_g[l], w_in[l], mu_rkv[l], mu_wa[l],
                    w0[l], w1[l], w2[l], a0[l], a1[l], a2[l], k_k[l], k_a[l], r_k[l],
                    ln_x_g[l], ln_x_b[l], rel_bias[l], w_branch_rwkv[l], w_branch_attn[l],
                    b_merge[l], w_out[l])
    return x2.reshape(b, s, d)
```

```python
import functools

import jax
import jax.numpy as jnp
import numpy as np
from jax import lax
from jax.experimental import pallas as pl
from jax.experimental.pallas import tpu as pltpu

F32 = jnp.float32
BF16 = jnp.bfloat16

D_MODEL = 2048
D_BRANCH = 1024
HEAD_DIM = 64
N_HEADS = 16
N_PAIRS = N_HEADS // 2
LANES = 128
CHUNK = 64
LORA = 64
DECAY_SCALE = 0.606531
PAST_CHUNKS = 8
REL_CLIP = 256
NORM_EPS = 1e-6
GN_EPS = 64e-5
NEG_INF = -1e30
D_IN = 8 * D_BRANCH + 2 * D_MODEL
D_IN_AUG = D_IN + 4 * LORA
VMEM_LIMIT = 56 * 1024 * 1024

ATT_QCH = 2
ATT_Q = ATT_QCH * CHUNK
ATT_KCH = PAST_CHUNKS + ATT_QCH
ATT_K = ATT_KCH * CHUNK
ATT_GROUP = 4


def _inproj_kernel(x_ref, g_ref, w_ref, o_ref, hn_ref):
    @pl.when(pl.program_id(1) == 0)
    def _():
        x = x_ref[...]
        ms = jnp.mean(x * x, axis=-1, keepdims=True)
        hn_ref[...] = (x * lax.rsqrt(ms + NORM_EPS) * g_ref[...]).astype(BF16)

    o_ref[...] = jnp.dot(hn_ref[...], w_ref[...],
                         preferred_element_type=F32).astype(BF16)


def _inproj(x2, g, w_aug, tm=512, tn=1792):
    m = x2.shape[0]
    n = w_aug.shape[1]
    return pl.pallas_call(
        _inproj_kernel,
        grid=(m // tm, n // tn),
        in_specs=[
            pl.BlockSpec((tm, D_MODEL), lambda i, j: (i, 0)),
            pl.BlockSpec((1, D_MODEL), lambda i, j: (0, 0)),
            pl.BlockSpec((D_MODEL, tn), lambda i, j: (0, j)),
        ],
        out_specs=pl.BlockSpec((tm, tn), lambda i, j: (i, j)),
        out_shape=jax.ShapeDtypeStruct((m, n), BF16),
        scratch_shapes=[pltpu.VMEM((tm, D_MODEL), BF16)],
        compiler_params=pltpu.CompilerParams(
            dimension_semantics=("parallel", "arbitrary"),
            vmem_limit_bytes=VMEM_LIMIT),
        name="inproj",
    )(x2, g, w_aug)


def _block_diag(x, lo_half):
    zero = jnp.zeros_like(x)
    return jnp.concatenate(
        [jnp.where(lo_half, x, zero), jnp.where(lo_half, zero, x)], axis=0)


def _rwkv_kernel(pr_ref, pk_ref, pv_ref, zr_ref, lo_ref,
                 mu_ref, w0_ref, a0_ref, kk_ref, ka_ref, rk_ref, lng_ref, lnb_ref,
                 w2a_ref, ones_ref, tri_ref,
                 o_ref,
                 br_ref, bk_ref, bv_ref, bl_ref, st_ref):
    c = pl.program_id(1)
    L = CHUNK

    @pl.when(c == 0)
    def _():
        br_ref[...] = jnp.zeros_like(br_ref)
        bk_ref[...] = jnp.zeros_like(bk_ref)
        bv_ref[...] = jnp.zeros_like(bv_ref)
        bl_ref[...] = jnp.zeros_like(bl_ref)
        st_ref[...] = jnp.zeros_like(st_ref)

    def shifted(buf_ref, cur):
        buf_ref[8:8 + L, :] = cur
        prev = buf_ref[7:7 + L, :]
        buf_ref[7:8, :] = cur[L - 1:L, :]
        return prev

    pr = pr_ref[...].astype(F32)
    pk = pk_ref[...].astype(F32)
    pv = pv_ref[...].astype(F32)
    r = pr + (shifted(br_ref, pr) - pr) * mu_ref[0:1, :]
    k = pk + (shifted(bk_ref, pk) - pk) * mu_ref[1:2, :]
    v = pv + (shifted(bv_ref, pv) - pv) * mu_ref[2:3, :]

    lo = lo_ref[...].astype(F32)
    hid = lo[:, :LANES] + shifted(bl_ref, lo[:, LANES:])
    lane1 = lax.broadcasted_iota(jnp.int32, (L, LANES), 1)
    hid = jnp.where(lane1 < LORA, jnp.tanh(hid), hid)
    dec = jnp.dot(hid.astype(BF16), w2a_ref[...], preferred_element_type=F32)
    lw = -DECAY_SCALE * jax.nn.sigmoid(w0_ref[...] + dec[:, :D_BRANCH])
    a = jax.nn.sigmoid(a0_ref[...] + dec[:, D_BRANCH:])

    ones_bd = ones_ref[...]

    def head_sum(x):
        xb = x.astype(BF16)
        return jnp.concatenate(
            [jnp.dot(xb[:, p * LANES:(p + 1) * LANES], ones_bd,
                     preferred_element_type=F32) for p in range(N_PAIRS)], axis=1)

    kk = k * kk_ref[...]
    nrm = jnp.sqrt(head_sum(kk * kk))
    kk = kk / jnp.maximum(nrm, 1e-12)
    k = k * (1.0 + (a - 1.0) * ka_ref[...])
    ap = -kk
    bp = kk * a

    tri = tri_ref[...]
    lw_hi = lw.astype(BF16)
    lw_lo = (lw - lw_hi.astype(F32)).astype(BF16)
    cum = (jnp.dot(tri, lw_hi, preferred_element_type=F32)
           + jnp.dot(tri, lw_lo, preferred_element_type=F32))
    cum_l = cum[L - 1:L, :]
    e_pos = jnp.exp(cum)
    e_neg = jnp.exp(-cum)
    e_end = jnp.exp(cum_l - cum)
    e_prev = jnp.exp(cum - lw)
    p_l = jnp.exp(cum_l)

    at_all = (ap * e_prev).astype(BF16)
    rt_all = (r * e_pos).astype(BF16)
    bt_all = (bp * e_neg).astype(BF16)
    kt_all = (k * e_neg).astype(BF16)
    bh_all = bp * e_end
    kh_all = k * e_end
    v_bf = v.astype(BF16)

    row = lax.broadcasted_iota(jnp.int32, (L, LANES), 0)
    li = lane1 & (HEAD_DIM - 1)
    strict = row > li
    incl = row >= li
    lo_half = lane1 < HEAD_DIM
    eye_p = (row == li).astype(F32)
    nt = (((1,), (1,)), ((), ()))

    pairs = range(N_PAIRS)
    sls = [slice(p * LANES, (p + 1) * LANES) for p in pairs]
    at = [at_all[:, sl] for sl in sls]
    rt = [rt_all[:, sl] for sl in sls]
    vp = [v_bf[:, sl] for sl in sls]
    st = [st_ref[p] for p in pairs]

    aa = [lax.dot_general(
        jnp.concatenate([at[p], rt[p]], axis=0),
        jnp.concatenate([_block_diag(bt_all[:, sls[p]], lo_half),
                         _block_diag(kt_all[:, sls[p]], lo_half)], axis=0),
        nt, preferred_element_type=F32) for p in pairs]
    a_ak = [jnp.where(strict, aa[p][:L, LANES:], 0.0).astype(BF16) for p in pairs]
    a_rb = [jnp.where(incl, aa[p][L:, :LANES], 0.0).astype(BF16) for p in pairs]
    a_rk = [jnp.where(incl, aa[p][L:, LANES:], 0.0).astype(BF16) for p in pairs]

    nm = [jnp.where(strict, aa[p][:L, :LANES], 0.0) for p in pairs]
    tp = [eye_p + nm[p] for p in pairs]
    for _ in range(5):
        nb = [nm[p].astype(BF16) for p in pairs]
        nm = [jnp.dot(nb[p], _block_diag(nb[p], lo_half), preferred_element_type=F32)
              for p in pairs]
        tp = [tp[p] + jnp.dot(nm[p].astype(BF16), _block_diag(tp[p].astype(BF16), lo_half),
                              preferred_element_type=F32) for p in pairs]

    st_bd = [_block_diag(st[p].astype(BF16), lo_half) for p in pairs]
    v_bd = [_block_diag(vp[p], lo_half) for p in pairs]
    x = [jnp.dot(jnp.concatenate([at[p], a_ak[p]], axis=1),
                 jnp.concatenate([st_bd[p], v_bd[p]], axis=0),
                 preferred_element_type=F32) for p in pairs]
    u_bf = [jnp.dot(tp[p].astype(BF16), _block_diag(x[p].astype(BF16), lo_half),
                    preferred_element_type=F32).astype(BF16) for p in pairs]
    ys = [jnp.dot(jnp.concatenate([rt[p], a_rb[p], a_rk[p]], axis=1),
                  jnp.concatenate([st_bd[p], _block_diag(u_bf[p], lo_half), v_bd[p]], axis=0),
                  preferred_element_type=F32) for p in pairs]

    for p in pairs:
        bkh_t = jnp.concatenate([bh_all[:, sls[p]], kh_all[:, sls[p]]], axis=0).T.astype(BF16)
        s_full = jnp.dot(bkh_t, jnp.concatenate([u_bf[p], vp[p]], axis=0),
                         preferred_element_type=F32)
        dcol = jnp.broadcast_to(p_l[:, sls[p]], (LANES, LANES)).T
        dp = jnp.where(lo_half, dcol[:L], dcol[L:])
        st_ref[p] = dp * st[p] + jnp.where(lo_half, s_full[:L], s_full[L:])

    y = jnp.concatenate(ys, axis=1)

    inv_n = 1.0 / HEAD_DIM
    mean = head_sum(y) * inv_n
    yc = y - mean
    var = head_sum(yc * yc) * inv_n
    yn = yc * lax.rsqrt(var + GN_EPS) * lng_ref[...] + lnb_ref[...]
    bonus = head_sum(r * k * rk_ref[...]) * v
    z = zr_ref[...].astype(F32)
    o_ref[...] = ((yn + bonus) * (z * jax.nn.sigmoid(z))).astype(BF16)


def _rwkv(proj, b, s, mu_rkv, w0, a0, k_k, k_a, r_k, ln_g, ln_b, w2a, ones_bd, tri):
    nc = s // CHUNK
    row = lambda bi, ci: bi * nc + ci
    col = lambda j: pl.BlockSpec((CHUNK, D_BRANCH), lambda bi, ci: (row(bi, ci), j))
    const = lambda shape: pl.BlockSpec(shape, lambda bi, ci: (0,) * len(shape))
    return pl.pallas_call(
        _rwkv_kernel,
        grid=(b, nc),
        in_specs=[
            col(0), col(1), col(2), col(3),
            pl.BlockSpec((CHUNK, 4 * LORA), lambda bi, ci: (row(bi, ci), D_IN // (4 * LORA))),
            const((3, D_BRANCH)), const((1, D_BRANCH)), const((1, D_BRANCH)),
            const((1, D_BRANCH)), const((1, D_BRANCH)), const((1, D_BRANCH)),
            const((1, D_BRANCH)), const((1, D_BRANCH)),
            const((2 * LORA, 2 * D_BRANCH)), const((LANES, LANES)), const((CHUNK, CHUNK)),
        ],
        out_specs=pl.BlockSpec((CHUNK, D_BRANCH), lambda bi, ci: (row(bi, ci), 0)),
        out_shape=jax.ShapeDtypeStruct((b * s, D_BRANCH), BF16),
        scratch_shapes=[
            pltpu.VMEM((8 + CHUNK, D_BRANCH), F32),
            pltpu.VMEM((8 + CHUNK, D_BRANCH), F32),
            pltpu.VMEM((8 + CHUNK, D_BRANCH), F32),
            pltpu.VMEM((8 + CHUNK, LANES), F32),
            pltpu.VMEM((N_PAIRS, HEAD_DIM, LANES), F32),
        ],
        compiler_params=pltpu.CompilerParams(
            dimension_semantics=("arbitrary", "arbitrary"),
            vmem_limit_bytes=VMEM_LIMIT),
        name="rwkv7",
    )(proj, proj, proj, proj, proj, mu_rkv, w0, a0, k_k, k_a, r_k, ln_g, ln_b, w2a, ones_bd, tri)


def _attn_kernel(*refs):
    q_ref = refs[0]
    k_refs = refs[1:1 + ATT_KCH]
    v_refs = refs[1 + ATT_KCH:1 + 2 * ATT_KCH]
    z_ref, bias_ref, o_ref = refs[1 + 2 * ATT_KCH:]
    i = pl.program_id(1)

    q = q_ref[...] * jnp.asarray(HEAD_DIM ** -0.5, BF16)
    kcat = jnp.concatenate([kr[...] for kr in k_refs], axis=0)
    vcat = jnp.concatenate([vr[...] for vr in v_refs], axis=0)

    klane = lax.broadcasted_iota(jnp.int32, (1, ATT_K), 1)
    kchunk = ATT_QCH * i - PAST_CHUNKS + klane // CHUNK
    start_mask = jnp.where(kchunk >= 0, 0.0, NEG_INF).astype(F32)

    lane = lax.broadcasted_iota(jnp.int32, (ATT_Q, LANES), 1)
    lo_half = lane < HEAD_DIM
    nt = (((1,), (1,)), ((), ()))
    outs = []
    for g0 in range(0, N_HEADS, ATT_GROUP):
        heads = range(g0, g0 + ATT_GROUP)
        sl = {h: slice((h // 2) * LANES, (h // 2 + 1) * LANES) for h in heads}
        qm = {h: jnp.where(lo_half if h % 2 == 0 else jnp.logical_not(lo_half),
                           q[:, sl[h]], jnp.zeros((ATT_Q, LANES), BF16)) for h in heads}
        s = {h: lax.dot_general(qm[h], kcat[:, sl[h]], nt, preferred_element_type=F32)
             + bias_ref[h] + start_mask for h in heads}
        m = {h: jnp.max(s[h], axis=-1, keepdims=True) for h in heads}
        e = {h: jnp.exp(s[h] - m[h]) for h in heads}
        l = {h: jnp.sum(e[h], axis=-1, keepdims=True) for h in heads}
        oh = {h: jnp.dot(e[h].astype(BF16), vcat[:, sl[h]], preferred_element_type=F32) / l[h]
              for h in heads}
        for h in heads[::2]:
            outs.append(jnp.where(lo_half, oh[h], oh[h + 1]))
    o = jnp.concatenate(outs, axis=1)
    z = z_ref[...].astype(F32)
    o_ref[...] = (o * (z * jax.nn.sigmoid(z))).astype(BF16)


def _attention(proj, b, s, bias):
    nq = s // ATT_Q
    nc = s // CHUNK
    qrow = lambda bi, qi: bi * nq + qi

    def kv_spec(colblk, j):
        def imap(bi, qi):
            return (bi * nc + jnp.maximum(ATT_QCH * qi - PAST_CHUNKS + j, 0), colblk)
        return pl.BlockSpec((CHUNK, D_BRANCH), imap)

    in_specs = ([pl.BlockSpec((ATT_Q, D_BRANCH), lambda bi, qi: (qrow(bi, qi), 4))]
                + [kv_spec(5, j) for j in range(ATT_KCH)]
                + [kv_spec(6, j) for j in range(ATT_KCH)]
                + [pl.BlockSpec((ATT_Q, D_BRANCH), lambda bi, qi: (qrow(bi, qi), 7)),
                   pl.BlockSpec((N_HEADS, ATT_Q, ATT_K), lambda bi, qi: (0, 0, 0))])
    return pl.pallas_call(
        _attn_kernel,
        grid=(b, nq),
        in_specs=in_specs,
        out_specs=pl.BlockSpec((ATT_Q, D_BRANCH), lambda bi, qi: (qrow(bi, qi), 0)),
        out_shape=jax.ShapeDtypeStruct((b * s, D_BRANCH), BF16),
        compiler_params=pltpu.CompilerParams(
            dimension_semantics=("parallel", "arbitrary"),
            vmem_limit_bytes=VMEM_LIMIT),
        name="band_attn",
    )(*([proj] * (2 + 2 * ATT_KCH)), bias)


BIAS_W = 768


def _bias_kernel(g_ref, o_ref):
    g = jnp.broadcast_to(g_ref[0], (ATT_Q, BIAS_W))
    t = pltpu.roll(g, 0, 1, stride=1, stride_axis=0)[:, :ATT_K]
    qc = lax.broadcasted_iota(jnp.int32, (ATT_Q, ATT_K), 0) // CHUNK
    kc = lax.broadcasted_iota(jnp.int32, (ATT_Q, ATT_K), 1) // CHUNK - PAST_CHUNKS
    valid = (kc <= qc) & (kc >= qc - PAST_CHUNKS)
    o_ref[0] = jnp.where(valid, t, NEG_INF)


def _attn_bias_table(rel_bias):
    m = np.arange(BIAS_W)
    m = np.where(m < ATT_K, m, m - BIAS_W)
    idx = np.clip(PAST_CHUNKS * CHUNK - m, -REL_CLIP, REL_CLIP) + REL_CLIP
    g = rel_bias[:, idx].astype(F32).reshape(N_HEADS, 1, BIAS_W)
    return pl.pallas_call(
        _bias_kernel,
        grid=(N_HEADS,),
        in_specs=[pl.BlockSpec((1, 1, BIAS_W), lambda h: (h, 0, 0))],
        out_specs=pl.BlockSpec((1, ATT_Q, ATT_K), lambda h: (h, 0, 0)),
        out_shape=jax.ShapeDtypeStruct((N_HEADS, ATT_Q, ATT_K), F32),
        name="rel_bias_table",
    )(g)


def _out_kernel(x_ref, yr_ref, ya_ref, mr_ref, ma_ref, wbr_ref, wba_ref, wout_ref,
                bm_ref, pg_ref, o_ref):
    u_r = jnp.dot(yr_ref[...], wbr_ref[...], preferred_element_type=F32)
    u_a = jnp.dot(ya_ref[...], wba_ref[...], preferred_element_type=F32)
    g_r = jax.nn.sigmoid(mr_ref[...].astype(F32) + bm_ref[0:1, :])
    g_a = jax.nn.sigmoid(ma_ref[...].astype(F32) + bm_ref[1:2, :])
    merged = (g_r * u_r + g_a * u_a).astype(BF16)
    o = jnp.dot(merged, wout_ref[...], preferred_element_type=F32)
    ms = jnp.mean(o * o, axis=-1, keepdims=True)
    o_ref[...] = x_ref[...] + o * lax.rsqrt(ms + NORM_EPS) * pg_ref[...]


def _outproj(x2, y_r, y_a, proj, w_br, w_ba, w_out, b_merge, post_g, tm=256):
    m = x2.shape[0]
    const = lambda shape: pl.BlockSpec(shape, lambda i: (0,) * len(shape),
                                       pipeline_mode=pl.Buffered(1))
    return pl.pallas_call(
        _out_kernel,
        grid=(m // tm,),
        in_specs=[
            pl.BlockSpec((tm, D_MODEL), lambda i: (i, 0)),
            pl.BlockSpec((tm, D_BRANCH), lambda i: (i, 0)),
            pl.BlockSpec((tm, D_BRANCH), lambda i: (i, 0)),
            pl.BlockSpec((tm, D_MODEL), lambda i: (i, 4)),
            pl.BlockSpec((tm, D_MODEL), lambda i: (i, 5)),
            const((D_BRANCH, D_MODEL)), const((D_BRANCH, D_MODEL)),
            const((D_MODEL, D_MODEL)), const((2, D_MODEL)), const((1, D_MODEL)),
        ],
        out_specs=pl.BlockSpec((tm, D_MODEL), lambda i: (i, 0)),
        out_shape=jax.ShapeDtypeStruct((m, D_MODEL), F32),
        compiler_params=pltpu.CompilerParams(
            dimension_semantics=("parallel",),
            vmem_limit_bytes=VMEM_LIMIT),
        name="merge_outproj",
    )(x2, y_r, y_a, proj, proj, w_br, w_ba, w_out, b_merge, post_g)


def _layer(x2, b, s, pre_g, post_g, w_in, mu_rkv, mu_wa, w0, w1, w2, a0, a1, a2,
           k_k, k_a, r_k, ln_g, ln_b, rel_bias, w_br, w_ba, b_merge, w_out):
    mw, ma = mu_wa[0][:, None], mu_wa[1][:, None]
    w_aug = jnp.concatenate(
        [w_in, (1.0 - mw) * w1, (1.0 - ma) * a1, mw * w1, ma * a1], axis=1).astype(BF16)
    zeros = jnp.zeros((LORA, D_BRANCH), F32)
    w2a = jnp.concatenate([jnp.concatenate([w2, zeros], axis=1),
                           jnp.concatenate([zeros, a2], axis=1)], axis=0).astype(BF16)
    lane = jnp.arange(LANES)
    ones_bd = (lane[:, None] // HEAD_DIM == lane[None, :] // HEAD_DIM).astype(BF16)
    tri = jnp.tril(jnp.ones((CHUNK, CHUNK), BF16))
    row = lambda t: t.reshape(1, -1).astype(F32)

    proj = _inproj(x2, row(pre_g), w_aug)
    y_r = _rwkv(proj, b, s, mu_rkv.astype(F32), row(w0), row(a0), row(k_k), row(k_a),
                row(r_k), row(ln_g), row(ln_b), w2a, ones_bd, tri)
    y_a = _attention(proj, b, s, _attn_bias_table(rel_bias))
    return _outproj(x2, y_r, y_a, proj, w_br.astype(BF16), w_ba.astype(BF16),
                    w_out.astype(BF16), b_merge.astype(F32), row(post_g))


def kernel(x, pre_norm_g, post_norm_g, w_in, mu_rkv, mu_wa, w0, w1, w2, a0, a1, a2,
           k_k, k_a, r_k, ln_x_g, ln_x_b, rel_bias, w_branch_rwkv, w_branch_attn,
           b_merge, w_out):
    b, s, d = x.shape
    assert d == D_MODEL and s % ATT_Q == 0 and (b * s) % 512 == 0
    x2 = x.reshape(b * s, d)
    for l in range(pre_norm_g.shape[0]):
        x2 = _layer(x2, b, s, pre_norm_g[l], post_norm_g[l], w_in[l], mu_rkv[l], mu_wa[l],
                    w0[l], w1[l], w2[l], a0[l], a1[l], a2[l], k_k[l], k_a[l], r_k[l],
                    ln_x_g[l], ln_x_b[l], rel_bias[l], w_branch_rwkv[l], w_branch_attn[l],
                    b_merge[l], w_out[l])
    return x2.reshape(b, s, d)
```

```python
import functools

import jax
import jax.numpy as jnp
import numpy as np
from jax import lax
from jax.experimental import pallas as pl
from jax.experimental.pallas import tpu as pltpu

F32 = jnp.float32
BF16 = jnp.bfloat16

D_MODEL = 2048
D_BRANCH = 1024
HEAD_DIM = 64
N_HEADS = 16
N_PAIRS = N_HEADS // 2
LANES = 128
CHUNK = 64
LORA = 64
DECAY_SCALE = 0.606531
PAST_CHUNKS = 8
REL_CLIP = 256
NORM_EPS = 1e-6
GN_EPS = 64e-5
NEG_INF = -1e30
LOG2E = 1.4426950408889634
D_IN = 8 * D_BRANCH + 2 * D_MODEL
VMEM_LIMIT = 56 * 1024 * 1024
INPROJ_TM = 1024
INPROJ_TN = 2048
OUT_TM = 256
RWKV_SUB = 2
PREV_ROWS = 16

ATT_QCH = 2
ATT_Q = ATT_QCH * CHUNK
ATT_KCH = PAST_CHUNKS + ATT_QCH
ATT_K = ATT_KCH * CHUNK
ATT_GROUP = 4


def _inproj_kernel(x_ref, g_ref, w_ref, wl_ref, o_ref, lo_ref, hn_ref):
    @pl.when(pl.program_id(1) == 0)
    def _():
        x = x_ref[...]
        ms = jnp.mean(x * x, axis=-1, keepdims=True)
        hn_ref[...] = (x * lax.rsqrt(ms + NORM_EPS) * g_ref[...]).astype(BF16)
        lo_ref[...] = jnp.dot(hn_ref[...], wl_ref[...],
                              preferred_element_type=F32).astype(BF16)

    o_ref[...] = jnp.dot(hn_ref[...], w_ref[...],
                         preferred_element_type=F32).astype(BF16)


def _inproj(x2, g, w_in, w_lora, tm=INPROJ_TM, tn=INPROJ_TN):
    m = x2.shape[0]
    n = w_in.shape[1]
    nl = w_lora.shape[1]
    return pl.pallas_call(
        _inproj_kernel,
        grid=(m // tm, n // tn),
        in_specs=[
            pl.BlockSpec((tm, D_MODEL), lambda i, j: (i, 0)),
            pl.BlockSpec((1, D_MODEL), lambda i, j: (0, 0)),
            pl.BlockSpec((D_MODEL, tn), lambda i, j: (0, j)),
            pl.BlockSpec((D_MODEL, nl), lambda i, j: (0, 0)),
        ],
        out_specs=[pl.BlockSpec((tm, tn), lambda i, j: (i, j)),
                   pl.BlockSpec((tm, nl), lambda i, j: (i, 0))],
        out_shape=[jax.ShapeDtypeStruct((m, n), BF16),
                   jax.ShapeDtypeStruct((m, nl), BF16)],
        scratch_shapes=[pltpu.VMEM((tm, D_MODEL), BF16)],
        compiler_params=pltpu.CompilerParams(
            dimension_semantics=("parallel", "arbitrary"),
            vmem_limit_bytes=VMEM_LIMIT),
        name="inproj",
    )(x2, g, w_in, w_lora)


def _block_diag(x, lo_half):
    zero = jnp.zeros_like(x)
    return jnp.concatenate(
        [jnp.where(lo_half, x, zero), jnp.where(lo_half, zero, x)], axis=0)


def _rwkv_kernel(pr_ref, pk_ref, pv_ref, zr_ref, lo_ref,
                 mu_ref, w0_ref, a0_ref, kk_ref, ka_ref, rk_ref, lng_ref, lnb_ref,
                 w2a_ref, ones_ref, tri_ref, shift_ref,
                 o_ref,
                 br_ref, bk_ref, bv_ref, bl_ref, st_ref):
    c = pl.program_id(1)
    L = CHUNK
    R = pr_ref.shape[0]
    subs = range(R // L)

    @pl.when(c == 0)
    def _():
        br_ref[...] = jnp.zeros_like(br_ref)
        bk_ref[...] = jnp.zeros_like(bk_ref)
        bv_ref[...] = jnp.zeros_like(bv_ref)
        bl_ref[...] = jnp.zeros_like(bl_ref)
        st_ref[...] = jnp.zeros_like(st_ref)

    def shifted(buf_ref, cur):
        ext = jnp.concatenate([buf_ref[...], cur], axis=0)
        buf_ref[...] = cur[R - PREV_ROWS:, :]
        return jnp.dot(shift_ref[...], ext, preferred_element_type=F32)

    lo = lo_ref[...]
    hid = lo[:, :LANES].astype(F32) + shifted(bl_ref, lo[:, LANES:])
    hid = jnp.where(lax.broadcasted_iota(jnp.int32, (R, LANES), 1) < LORA, jnp.tanh(hid), hid)
    dec = jnp.dot(hid.astype(BF16), w2a_ref[...], preferred_element_type=F32)
    lw = -DECAY_SCALE * jax.nn.sigmoid(w0_ref[...] + dec[:, :D_BRANCH])
    a = jax.nn.sigmoid(a0_ref[...] + dec[:, D_BRANCH:])

    pr, pk, pv = pr_ref[...], pk_ref[...], pv_ref[...]
    sr, sk, sv_ = shifted(br_ref, pr), shifted(bk_ref, pk), shifted(bv_ref, pv)
    pr, pk, pv = pr.astype(F32), pk.astype(F32), pv.astype(F32)
    r = pr + (sr - pr) * mu_ref[0:1, :]
    k = pk + (sk - pk) * mu_ref[1:2, :]
    v = pv + (sv_ - pv) * mu_ref[2:3, :]

    ones_bd = ones_ref[...]

    pairs = range(N_PAIRS)
    sls = [slice(p * LANES, (p + 1) * LANES) for p in pairs]

    def head_sum(x):
        n = x.shape[0]
        xs = jnp.concatenate([x[:, sl] for sl in sls], axis=0).astype(BF16)
        s = jnp.dot(xs, ones_bd, preferred_element_type=F32)
        return jnp.concatenate([s[p * n:(p + 1) * n] for p in pairs], axis=1)

    kk = k * kk_ref[...]
    kk = kk * jnp.minimum(lax.rsqrt(head_sum(kk * kk)), 1e12)
    k = k * (1.0 + (a - 1.0) * ka_ref[...])
    ap = -kk
    bp = kk * a

    lw_hi = lw.astype(BF16)
    lw_lo = (lw - lw_hi.astype(F32)).astype(BF16)
    v_bf = v.astype(BF16)

    lane1 = lax.broadcasted_iota(jnp.int32, (L, LANES), 1)
    row = lax.broadcasted_iota(jnp.int32, (L, LANES), 0)
    li = lane1 & (HEAD_DIM - 1)
    strict = row > li
    incl = row >= li
    lo_half = lane1 < HEAD_DIM
    eye_p = (row == li).astype(F32)
    nt = (((1,), (1,)), ((), ()))

    at, rt, vp, bkh, a_ak, a_rb, a_rk, nm, p_l = ({} for _ in range(9))
    for j in subs:
        rs = slice(j * L, (j + 1) * L)
        cum = jnp.dot(tri_ref[...], jnp.concatenate([lw_hi[rs], lw_lo[rs]], axis=0),
                      preferred_element_type=F32)
        e_pos = jnp.exp(cum)
        e_neg = 1.0 / e_pos
        p_l[j] = e_pos[L - 1:L, :]
        e_end = p_l[j] * e_neg
        at_all = (ap[rs] * jnp.exp(cum - lw[rs])).astype(BF16)
        rt_all = (r[rs] * e_pos).astype(BF16)
        bt_all = (bp[rs] * e_neg).astype(BF16)
        kt_all = (k[rs] * e_neg).astype(BF16)
        bh_all = bp[rs] * e_end
        kh_all = k[rs] * e_end
        for p in pairs:
            at[j, p], rt[j, p], vp[j, p] = at_all[:, sls[p]], rt_all[:, sls[p]], v_bf[rs, sls[p]]
            bkh[j, p] = jnp.concatenate([bh_all[:, sls[p]], kh_all[:, sls[p]]], axis=0)
            aa = lax.dot_general(
                jnp.concatenate([at[j, p], rt[j, p]], axis=0),
                jnp.concatenate([_block_diag(bt_all[:, sls[p]], lo_half),
                                 _block_diag(kt_all[:, sls[p]], lo_half)], axis=0),
                nt, preferred_element_type=F32)
            nm[j, p] = jnp.where(strict, aa[:L, :LANES], 0.0)
            a_ak[j, p] = jnp.where(strict, aa[:L, LANES:], 0.0).astype(BF16)
            a_rb[j, p] = jnp.where(incl, aa[L:, :LANES], 0.0).astype(BF16)
            a_rk[j, p] = jnp.where(incl, aa[L:, LANES:], 0.0).astype(BF16)
    units = list(nm)

    tp = {u: eye_p + nm[u] for u in units}
    nb = {u: nm[u].astype(BF16) for u in units}
    nm = {u: jnp.dot(nb[u], _block_diag(nb[u], lo_half), preferred_element_type=F32)
          for u in units}
    for _ in range(4):
        nb = {u: nm[u].astype(BF16) for u in units}
        tn = {u: jnp.dot(jnp.concatenate([tp[u].astype(BF16), nb[u]], axis=0),
                         _block_diag(nb[u], lo_half), preferred_element_type=F32)
              for u in units}
        tp = {u: tp[u] + tn[u][:L] for u in units}
        nm = {u: tn[u][L:] for u in units}
    tp = {u: (tp[u] + jnp.dot(tp[u].astype(BF16), _block_diag(nm[u].astype(BF16), lo_half),
                              preferred_element_type=F32)).astype(BF16) for u in units}
    bkh_t = {u: bkh[u].T.astype(BF16) for u in units}

    st = [st_ref[p] for p in pairs]
    ys = []
    for j in subs:
        sv = [jnp.concatenate([_block_diag(st[p].astype(BF16), lo_half),
                               _block_diag(vp[j, p], lo_half)], axis=0) for p in pairs]
        xy = [jnp.dot(jnp.concatenate([jnp.concatenate([at[j, p], a_ak[j, p]], axis=1),
                                       jnp.concatenate([rt[j, p], a_rk[j, p]], axis=1)], axis=0),
                      sv[p], preferred_element_type=F32) for p in pairs]
        u_bf = [jnp.dot(tp[j, p], _block_diag(xy[p][:L].astype(BF16), lo_half),
                        preferred_element_type=F32).astype(BF16) for p in pairs]
        ys.append(jnp.concatenate(
            [xy[p][L:] + jnp.dot(a_rb[j, p], _block_diag(u_bf[p], lo_half),
                                 preferred_element_type=F32) for p in pairs], axis=1))
        for p in pairs:
            s_full = jnp.dot(bkh_t[j, p], jnp.concatenate([u_bf[p], vp[j, p]], axis=0),
                             preferred_element_type=F32)
            dcol = jnp.broadcast_to(p_l[j][:, sls[p]], (LANES, LANES)).T
            dp = jnp.where(lo_half, dcol[:L], dcol[L:])
            st[p] = dp * st[p] + jnp.where(lo_half, s_full[:L], s_full[L:])
    for p in pairs:
        st_ref[p] = st[p]

    y = jnp.concatenate(ys, axis=0)

    inv_n = 1.0 / HEAD_DIM
    mean = head_sum(y) * inv_n
    yc = y - mean
    var = head_sum(yc * yc) * inv_n
    yn = yc * lax.rsqrt(var + GN_EPS) * lng_ref[...] + lnb_ref[...]
    bonus = head_sum(r * k * rk_ref[...]) * v
    z = zr_ref[...].astype(F32)
    o_ref[...] = ((yn + bonus) * (z * jax.nn.sigmoid(z))).astype(BF16)


def _rwkv(proj, lora, b, s, mu_rkv, w0, a0, k_k, k_a, r_k, ln_g, ln_b, w2a, ones_bd, tri):
    rows = RWKV_SUB * CHUNK
    nc = s // rows
    shift = jnp.eye(rows, PREV_ROWS + rows, k=PREV_ROWS - 1, dtype=BF16)
    row = lambda bi, ci: bi * nc + ci
    col = lambda j: pl.BlockSpec((rows, D_BRANCH), lambda bi, ci: (row(bi, ci), j))
    const = lambda shape: pl.BlockSpec(shape, lambda bi, ci: (0,) * len(shape))
    return pl.pallas_call(
        _rwkv_kernel,
        grid=(b, nc),
        in_specs=[
            col(0), col(1), col(2), col(3),
            pl.BlockSpec((rows, 4 * LORA), lambda bi, ci: (row(bi, ci), 0)),
            const((3, D_BRANCH)), const((1, D_BRANCH)), const((1, D_BRANCH)),
            const((1, D_BRANCH)), const((1, D_BRANCH)), const((1, D_BRANCH)),
            const((1, D_BRANCH)), const((1, D_BRANCH)),
            const((2 * LORA, 2 * D_BRANCH)), const((LANES, LANES)), const((CHUNK, 2 * CHUNK)),
            const((rows, PREV_ROWS + rows)),
        ],
        out_specs=pl.BlockSpec((rows, D_BRANCH), lambda bi, ci: (row(bi, ci), 0)),
        out_shape=jax.ShapeDtypeStruct((b * s, D_BRANCH), BF16),
        scratch_shapes=[
            pltpu.VMEM((PREV_ROWS, D_BRANCH), BF16),
            pltpu.VMEM((PREV_ROWS, D_BRANCH), BF16),
            pltpu.VMEM((PREV_ROWS, D_BRANCH), BF16),
            pltpu.VMEM((PREV_ROWS, LANES), BF16),
            pltpu.VMEM((N_PAIRS, HEAD_DIM, LANES), F32),
        ],
        compiler_params=pltpu.CompilerParams(
            dimension_semantics=("arbitrary", "arbitrary"),
            vmem_limit_bytes=VMEM_LIMIT),
        name="rwkv7",
    )(proj, proj, proj, proj, lora, mu_rkv, w0, a0, k_k, k_a, r_k, ln_g, ln_b, w2a, ones_bd, tri,
      shift)


def _attn_kernel(*refs):
    q_ref = refs[0]
    k_refs = refs[1:1 + ATT_KCH]
    v_refs = refs[1 + ATT_KCH:1 + 2 * ATT_KCH]
    z_ref, bias_ref, o_ref = refs[1 + 2 * ATT_KCH:]
    i = pl.program_id(1)

    def body(mask_start):
        q = (q_ref[...].astype(F32) * (HEAD_DIM ** -0.5 * LOG2E)).astype(BF16)
        kcat = jnp.concatenate([kr[...] for kr in k_refs], axis=0)
        vcat = jnp.concatenate([vr[...] for vr in v_refs], axis=0)

        if mask_start:
            klane = lax.broadcasted_iota(jnp.int32, (1, ATT_K), 1)
            kchunk = ATT_QCH * i - PAST_CHUNKS + klane // CHUNK
            start_mask = jnp.where(kchunk >= 0, 0.0, NEG_INF).astype(F32)

        lane = lax.broadcasted_iota(jnp.int32, (ATT_Q, LANES), 1)
        lo_half = lane < HEAD_DIM
        nt = (((1,), (1,)), ((), ()))
        outs = []
        for g0 in range(0, N_HEADS, ATT_GROUP):
            heads = range(g0, g0 + ATT_GROUP)
            sl = {h: slice((h // 2) * LANES, (h // 2 + 1) * LANES) for h in heads}
            qm = {h: jnp.where(lo_half if h % 2 == 0 else jnp.logical_not(lo_half),
                               q[:, sl[h]], jnp.zeros((ATT_Q, LANES), BF16)) for h in heads}
            s = {h: lax.dot_general(qm[h], kcat[:, sl[h]], nt, preferred_element_type=F32)
                 + bias_ref[h] for h in heads}
            if mask_start:
                s = {h: s[h] + start_mask for h in heads}
            m = {h: jnp.max(s[h], axis=-1, keepdims=True) for h in heads}
            e = {h: jnp.exp2(s[h] - m[h]) for h in heads}
            l = {h: jnp.sum(e[h], axis=-1, keepdims=True) for h in heads}
            oh = {h: jnp.dot(e[h].astype(BF16), vcat[:, sl[h]],
                             preferred_element_type=F32) / l[h] for h in heads}
            for h in heads[::2]:
                outs.append(jnp.where(lo_half, oh[h], oh[h + 1]))
        o = jnp.concatenate(outs, axis=1)
        z = z_ref[...].astype(F32)
        o_ref[...] = (o * (z * jax.nn.sigmoid(z))).astype(BF16)

    first_blocks = PAST_CHUNKS // ATT_QCH
    pl.when(i < first_blocks)(functools.partial(body, True))
    pl.when(i >= first_blocks)(functools.partial(body, False))


def _attention(proj, b, s, bias):
    nq = s // ATT_Q
    nc = s // CHUNK
    qrow = lambda bi, qi: bi * nq + qi

    def kv_spec(colblk, j):
        def imap(bi, qi):
            return (bi * nc + jnp.maximum(ATT_QCH * qi - PAST_CHUNKS + j, 0), colblk)
        return pl.BlockSpec((CHUNK, D_BRANCH), imap)

    in_specs = ([pl.BlockSpec((ATT_Q, D_BRANCH), lambda bi, qi: (qrow(bi, qi), 4))]
                + [kv_spec(5, j) for j in range(ATT_KCH)]
                + [kv_spec(6, j) for j in range(ATT_KCH)]
                + [pl.BlockSpec((ATT_Q, D_BRANCH), lambda bi, qi: (qrow(bi, qi), 7)),
                   pl.BlockSpec((N_HEADS, ATT_Q, ATT_K), lambda bi, qi: (0, 0, 0))])
    return pl.pallas_call(
        _attn_kernel,
        grid=(b, nq),
        in_specs=in_specs,
        out_specs=pl.BlockSpec((ATT_Q, D_BRANCH), lambda bi, qi: (qrow(bi, qi), 0)),
        out_shape=jax.ShapeDtypeStruct((b * s, D_BRANCH), BF16),
        compiler_params=pltpu.CompilerParams(
            dimension_semantics=("parallel", "arbitrary"),
            vmem_limit_bytes=VMEM_LIMIT),
        name="band_attn",
    )(*([proj] * (2 + 2 * ATT_KCH)), bias)


BIAS_W = 768


def _bias_kernel(g_ref, o_ref):
    g = jnp.broadcast_to(g_ref[0], (ATT_Q, BIAS_W))
    t = pltpu.roll(g, 0, 1, stride=1, stride_axis=0)[:, :ATT_K]
    qc = lax.broadcasted_iota(jnp.int32, (ATT_Q, ATT_K), 0) // CHUNK
    kc = lax.broadcasted_iota(jnp.int32, (ATT_Q, ATT_K), 1) // CHUNK - PAST_CHUNKS
    valid = (kc <= qc) & (kc >= qc - PAST_CHUNKS)
    o_ref[0] = jnp.where(valid, t * LOG2E, NEG_INF)


def _attn_bias_table(rel_bias):
    m = np.arange(BIAS_W)
    m = np.where(m < ATT_K, m, m - BIAS_W)
    idx = np.clip(PAST_CHUNKS * CHUNK - m, -REL_CLIP, REL_CLIP) + REL_CLIP
    g = rel_bias[:, idx].astype(F32).reshape(N_HEADS, 1, BIAS_W)
    return pl.pallas_call(
        _bias_kernel,
        grid=(N_HEADS,),
        in_specs=[pl.BlockSpec((1, 1, BIAS_W), lambda h: (h, 0, 0))],
        out_specs=pl.BlockSpec((1, ATT_Q, ATT_K), lambda h: (h, 0, 0)),
        out_shape=jax.ShapeDtypeStruct((N_HEADS, ATT_Q, ATT_K), F32),
        name="rel_bias_table",
    )(g)


def _out_kernel(x_ref, yr_ref, ya_ref, mr_ref, ma_ref, wbr_ref, wba_ref, wout_ref,
                bm_ref, pg_ref, o_ref):
    half = x_ref.shape[0] // 2
    rows = [slice(0, half), slice(half, 2 * half)]
    u_r = [jnp.dot(yr_ref[r, :], wbr_ref[...], preferred_element_type=F32) for r in rows]
    u_a = [jnp.dot(ya_ref[r, :], wba_ref[...], preferred_element_type=F32) for r in rows]
    merged = []
    for h, r in enumerate(rows):
        g_r = jax.nn.sigmoid(mr_ref[r, :].astype(F32) + bm_ref[0:1, :])
        g_a = jax.nn.sigmoid(ma_ref[r, :].astype(F32) + bm_ref[1:2, :])
        merged.append((g_r * u_r[h] + g_a * u_a[h]).astype(BF16))
    o = [jnp.dot(mg, wout_ref[...], preferred_element_type=F32) for mg in merged]
    for h, r in enumerate(rows):
        ms = jnp.mean(o[h] * o[h], axis=-1, keepdims=True)
        o_ref[r, :] = x_ref[r, :] + o[h] * lax.rsqrt(ms + NORM_EPS) * pg_ref[...]


def _outproj(x2, y_r, y_a, proj, w_br, w_ba, w_out, b_merge, post_g, tm=OUT_TM):
    m = x2.shape[0]
    const = lambda shape: pl.BlockSpec(shape, lambda i: (0,) * len(shape),
                                       pipeline_mode=pl.Buffered(1))
    return pl.pallas_call(
        _out_kernel,
        grid=(m // tm,),
        in_specs=[
            pl.BlockSpec((tm, D_MODEL), lambda i: (i, 0)),
            pl.BlockSpec((tm, D_BRANCH), lambda i: (i, 0)),
            pl.BlockSpec((tm, D_BRANCH), lambda i: (i, 0)),
            pl.BlockSpec((tm, D_MODEL), lambda i: (i, 4)),
            pl.BlockSpec((tm, D_MODEL), lambda i: (i, 5)),
            const((D_BRANCH, D_MODEL)), const((D_BRANCH, D_MODEL)),
            const((D_MODEL, D_MODEL)), const((2, D_MODEL)), const((1, D_MODEL)),
        ],
        out_specs=pl.BlockSpec((tm, D_MODEL), lambda i: (i, 0)),
        out_shape=jax.ShapeDtypeStruct((m, D_MODEL), F32),
        compiler_params=pltpu.CompilerParams(
            dimension_semantics=("parallel",),
            vmem_limit_bytes=VMEM_LIMIT),
        name="merge_outproj",
    )(x2, y_r, y_a, proj, proj, w_br, w_ba, w_out, b_merge, post_g)


def _layer(x2, b, s, pre_g, post_g, w_in, mu_rkv, mu_wa, w0, w1, w2, a0, a1, a2,
           k_k, k_a, r_k, ln_g, ln_b, rel_bias, w_br, w_ba, b_merge, w_out):
    mw, ma = mu_wa[0][:, None], mu_wa[1][:, None]
    w_lora = jnp.concatenate(
        [(1.0 - mw) * w1, (1.0 - ma) * a1, mw * w1, ma * a1], axis=1).astype(BF16)
    zeros = jnp.zeros((LORA, D_BRANCH), F32)
    w2a = jnp.concatenate([jnp.concatenate([w2, zeros], axis=1),
                           jnp.concatenate([zeros, a2], axis=1)], axis=0).astype(BF16)
    lane = jnp.arange(LANES)
    ones_bd = (lane[:, None] // HEAD_DIM == lane[None, :] // HEAD_DIM).astype(BF16)
    tri = jnp.tril(jnp.ones((CHUNK, CHUNK), BF16))
    tri = jnp.concatenate([tri, tri], axis=1)
    row = lambda t: t.reshape(1, -1).astype(F32)

    proj, lora = _inproj(x2, row(pre_g), w_in.astype(BF16), w_lora)
    y_r = _rwkv(proj, lora, b, s, mu_rkv.astype(F32), row(w0), row(a0), row(k_k), row(k_a),
                row(r_k), row(ln_g), row(ln_b), w2a, ones_bd, tri)
    y_a = _attention(proj, b, s, _attn_bias_table(rel_bias))
    return _outproj(x2, y_r, y_a, proj, w_br.astype(BF16), w_ba.astype(BF16),
                    w_out.astype(BF16), b_merge.astype(F32), row(post_g))


def kernel(x, pre_norm_g, post_norm_g, w_in, mu_rkv, mu_wa, w0, w1, w2, a0, a1, a2,
           k_k, k_a, r_k, ln_x_g, ln_x_b, rel_bias, w_branch_rwkv, w_branch_attn,
           b_merge, w_out):
    b, s, d = x.shape
    assert d == D_MODEL and s % ATT_Q == 0 and (b * s) % INPROJ_TM == 0
    x2 = x.reshape(b * s, d)
    for l in range(pre_norm_g.shape[0]):
        x2 = _layer(x2, b, s, pre_norm_g[l], post_norm_g[l], w_in[l], mu_rkv[l], mu_wa[l],
                    w0[l], w1[l], w2[l], a0[l], a1[l], a2[l], k_k[l], k_a[l], r_k[l],
                    ln_x_g[l], ln_x_b[l], rel_bias[l], w_branch_rwkv[l], w_branch_attn[l],
                    b_merge[l], w_out[l])
    return x2.reshape(b, s, d)
```

```python
import functools

import jax
import jax.numpy as jnp
import numpy as np
from jax import lax
from jax.experimental import pallas as pl
from jax.experimental.pallas import tpu as pltpu

F32 = jnp.float32
BF16 = jnp.bfloat16

D_MODEL = 2048
D_BRANCH = 1024
HEAD_DIM = 64
N_HEADS = 16
N_PAIRS = N_HEADS // 2
LANES = 128
CHUNK = 64
LORA = 64
DECAY_SCALE = 0.606531
PAST_CHUNKS = 8
REL_CLIP = 256
NORM_EPS = 1e-6
GN_EPS = 64e-5
NEG_INF = -1e30
LOG2E = 1.4426950408889634
D_IN = 8 * D_BRANCH + 2 * D_MODEL
VMEM_LIMIT = 56 * 1024 * 1024
INPROJ_TM = 1024
INPROJ_TN = 2048
OUT_TM = 512
RWKV_SUB = 2
PREV_ROWS = 16
STAGED = ("at", "rt", "vp", "a_ak", "a_rb", "a_rk", "tp")
RWKV_ORDER = "FBFBFBFBFBFBFBFBFBFF"

ATT_QCH = 2
ATT_Q = ATT_QCH * CHUNK
ATT_KCH = PAST_CHUNKS + ATT_QCH
ATT_K = ATT_KCH * CHUNK
ATT_GROUP = 4


def _inproj_kernel(x_ref, g_ref, w_ref, wl_ref, o_ref, lo_ref, hn_ref):
    @pl.when(pl.program_id(1) == 0)
    def _():
        x = x_ref[...]
        ms = jnp.mean(x * x, axis=-1, keepdims=True)
        hn_ref[...] = (x * lax.rsqrt(ms + NORM_EPS) * g_ref[...]).astype(BF16)
        lo_ref[...] = jnp.dot(hn_ref[...], wl_ref[...],
                              preferred_element_type=F32).astype(BF16)

    o_ref[...] = jnp.dot(hn_ref[...], w_ref[...],
                         preferred_element_type=F32).astype(BF16)


def _inproj(x2, g, w_in, w_lora, tm=INPROJ_TM, tn=INPROJ_TN):
    m = x2.shape[0]
    n = w_in.shape[1]
    nl = w_lora.shape[1]
    return pl.pallas_call(
        _inproj_kernel,
        grid=(m // tm, n // tn),
        in_specs=[
            pl.BlockSpec((tm, D_MODEL), lambda i, j: (i, 0)),
            pl.BlockSpec((1, D_MODEL), lambda i, j: (0, 0)),
            pl.BlockSpec((D_MODEL, tn), lambda i, j: (0, j)),
            pl.BlockSpec((D_MODEL, nl), lambda i, j: (0, 0)),
        ],
        out_specs=[pl.BlockSpec((tm, tn), lambda i, j: (i, j)),
                   pl.BlockSpec((tm, nl), lambda i, j: (i, 0))],
        out_shape=[jax.ShapeDtypeStruct((m, n), BF16),
                   jax.ShapeDtypeStruct((m, nl), BF16)],
        scratch_shapes=[pltpu.VMEM((tm, D_MODEL), BF16)],
        compiler_params=pltpu.CompilerParams(
            dimension_semantics=("parallel", "arbitrary"),
            vmem_limit_bytes=VMEM_LIMIT),
        name="inproj",
    )(x2, g, w_in, w_lora)


def _block_diag(x, lo_half):
    zero = jnp.zeros_like(x)
    return jnp.concatenate(
        [jnp.where(lo_half, x, zero), jnp.where(lo_half, zero, x)], axis=0)


def _rwkv_kernel(pr_ref, pk_ref, pv_ref, zr_ref, lo_ref,
                 mu_ref, w0_ref, a0_ref, kk_ref, ka_ref, rk_ref, lng_ref, lnb_ref,
                 w2a_ref, ones_ref, tri_ref, shift_ref,
                 o_ref,
                 br_ref, bk_ref, bv_ref, bl_ref, st_ref, s16_ref, sbk_ref, sdp_ref, sfin_ref):
    t = pl.program_id(1)
    L = CHUNK
    R = pr_ref.shape[0]
    subs = range(R // L)
    pairs = range(N_PAIRS)
    sls = [slice(p * LANES, (p + 1) * LANES) for p in pairs]
    units = [(j, p) for j in subs for p in pairs]

    @pl.when(t == 0)
    def _():
        for ref in (br_ref, bk_ref, bv_ref, bl_ref, st_ref, s16_ref, sbk_ref, sdp_ref, sfin_ref):
            ref[...] = jnp.zeros_like(ref)

    ones_bd = ones_ref[...]
    lane1 = lax.broadcasted_iota(jnp.int32, (L, LANES), 1)
    row = lax.broadcasted_iota(jnp.int32, (L, LANES), 0)
    li = lane1 & (HEAD_DIM - 1)
    strict = row > li
    incl = row >= li
    lo_half = lane1 < HEAD_DIM
    eye_p = (row == li).astype(F32)
    nt = (((1,), (1,)), ((), ()))
    inv_n = 1.0 / HEAD_DIM

    def head_sum(x):
        n = x.shape[0]
        xs = jnp.concatenate([x[:, sl] for sl in sls], axis=0).astype(BF16)
        s = jnp.dot(xs, ones_bd, preferred_element_type=F32)
        return jnp.concatenate([s[p * n:(p + 1) * n] for p in pairs], axis=1)

    def shifted(buf_ref, cur):
        ext = jnp.concatenate([buf_ref[...], cur], axis=0)
        buf_ref[...] = cur[R - PREV_ROWS:, :]
        return jnp.dot(shift_ref[...], ext, preferred_element_type=F32)

    staged = {name: {(j, p): s16_ref[j, i, :, sls[p]] for (j, p) in units}
              for i, name in enumerate(STAGED)}
    g_bkh = {(j, p): sbk_ref[j, :, sls[p]] for (j, p) in units}
    g_dp = {(j, p): sdp_ref[j, :, sls[p]] for (j, p) in units}
    g_bonus, g_gate = sfin_ref[0], sfin_ref[1]

    def front():
        lo = lo_ref[...]
        hid = lo[:, :LANES].astype(F32) + shifted(bl_ref, lo[:, LANES:])
        hid = jnp.where(lax.broadcasted_iota(jnp.int32, (R, LANES), 1) < LORA,
                        jnp.tanh(hid), hid)
        dec = jnp.dot(hid.astype(BF16), w2a_ref[...], preferred_element_type=F32)
        yield
        pr, pk, pv = pr_ref[...], pk_ref[...], pv_ref[...]
        sr, sk, sv_ = shifted(br_ref, pr), shifted(bk_ref, pk), shifted(bv_ref, pv)
        yield
        lw = -DECAY_SCALE * jax.nn.sigmoid(w0_ref[...] + dec[:, :D_BRANCH])
        a = jax.nn.sigmoid(a0_ref[...] + dec[:, D_BRANCH:])
        pr, pk, pv = pr.astype(F32), pk.astype(F32), pv.astype(F32)
        r = pr + (sr - pr) * mu_ref[0:1, :]
        k = pk + (sk - pk) * mu_ref[1:2, :]
        v = pv + (sv_ - pv) * mu_ref[2:3, :]
        kk = k * kk_ref[...]
        kk_ss = head_sum(kk * kk)
        k = k * (1.0 + (a - 1.0) * ka_ref[...])
        rk_sum = head_sum(r * k * rk_ref[...])
        yield
        kk = kk * jnp.minimum(lax.rsqrt(kk_ss), 1e12)
        ap = -kk
        bp = kk * a
        z = zr_ref[...].astype(F32)
        sfin_ref[0] = rk_sum * v
        sfin_ref[1] = z * jax.nn.sigmoid(z)
        lw_hi = lw.astype(BF16)
        lw_lo = (lw - lw_hi.astype(F32)).astype(BF16)
        v_bf = v.astype(BF16)
        cums = [jnp.dot(tri_ref[...],
                        jnp.concatenate([lw_hi[j * L:(j + 1) * L], lw_lo[j * L:(j + 1) * L]],
                                        axis=0),
                        preferred_element_type=F32) for j in subs]
        yield
        nm, a_ak, a_rb, a_rk = {}, {}, {}, {}
        for j in subs:
            rs = slice(j * L, (j + 1) * L)
            cum = cums[j]
            e_pos = jnp.exp(cum)
            e_neg = 1.0 / e_pos
            p_l = e_pos[L - 1:L, :]
            e_end = p_l * e_neg
            at_all = (ap[rs] * jnp.exp(cum - lw[rs])).astype(BF16)
            rt_all = (r[rs] * e_pos).astype(BF16)
            bt_all = (bp[rs] * e_neg).astype(BF16)
            kt_all = (k[rs] * e_neg).astype(BF16)
            bh_all = bp[rs] * e_end
            kh_all = k[rs] * e_end
            s16_ref[j, STAGED.index("at")] = at_all
            s16_ref[j, STAGED.index("rt")] = rt_all
            s16_ref[j, STAGED.index("vp")] = v_bf[rs]
            bkh_t, dps = [], []
            for p in pairs:
                bkh = jnp.concatenate([bh_all[:, sls[p]], kh_all[:, sls[p]]], axis=0)
                bkh_t.append(bkh.T.astype(BF16))
                dcol = jnp.broadcast_to(p_l[:, sls[p]], (LANES, LANES)).T
                dps.append(jnp.where(lo_half, dcol[:L], dcol[L:]))
                aa = lax.dot_general(
                    jnp.concatenate([at_all[:, sls[p]], rt_all[:, sls[p]]], axis=0),
                    jnp.concatenate([_block_diag(bt_all[:, sls[p]], lo_half),
                                     _block_diag(kt_all[:, sls[p]], lo_half)], axis=0),
                    nt, preferred_element_type=F32)
                nm[j, p] = jnp.where(strict, aa[:L, :LANES], 0.0)
                a_ak[j, p] = jnp.where(strict, aa[:L, LANES:], 0.0).astype(BF16)
                a_rb[j, p] = jnp.where(incl, aa[L:, :LANES], 0.0).astype(BF16)
                a_rk[j, p] = jnp.where(incl, aa[L:, LANES:], 0.0).astype(BF16)
            sbk_ref[j] = jnp.concatenate(bkh_t, axis=1)
            sdp_ref[j] = jnp.concatenate(dps, axis=1)
            for name, d in (("a_ak", a_ak), ("a_rb", a_rb), ("a_rk", a_rk)):
                s16_ref[j, STAGED.index(name)] = jnp.concatenate([d[j, p] for p in pairs], axis=1)
        yield
        tp = {u: eye_p + nm[u] for u in units}
        nb = {u: nm[u].astype(BF16) for u in units}
        nm = {u: jnp.dot(nb[u], _block_diag(nb[u], lo_half), preferred_element_type=F32)
              for u in units}
        yield
        for _ in range(4):
            nb = {u: nm[u].astype(BF16) for u in units}
            tn = {u: jnp.dot(jnp.concatenate([tp[u].astype(BF16), nb[u]], axis=0),
                             _block_diag(nb[u], lo_half), preferred_element_type=F32)
                  for u in units}
            tp = {u: tp[u] + tn[u][:L] for u in units}
            nm = {u: tn[u][L:] for u in units}
            yield
        tp = {u: (tp[u] + jnp.dot(tp[u].astype(BF16), _block_diag(nm[u].astype(BF16), lo_half),
                                  preferred_element_type=F32)).astype(BF16) for u in units}
        for j in subs:
            s16_ref[j, STAGED.index("tp")] = jnp.concatenate([tp[j, p] for p in pairs], axis=1)

    def back():
        at, rt, vp = staged["at"], staged["rt"], staged["vp"]
        a_ak, a_rb, a_rk, tp = staged["a_ak"], staged["a_rb"], staged["a_rk"], staged["tp"]
        st = [st_ref[p] for p in pairs]
        ys = []
        for j in subs:
            sv = [jnp.concatenate([_block_diag(st[p].astype(BF16), lo_half),
                                   _block_diag(vp[j, p], lo_half)], axis=0) for p in pairs]
            xy = [jnp.dot(jnp.concatenate([jnp.concatenate([at[j, p], a_ak[j, p]], axis=1),
                                           jnp.concatenate([rt[j, p], a_rk[j, p]], axis=1)],
                                          axis=0),
                          sv[p], preferred_element_type=F32) for p in pairs]
            yield
            u_bf = [jnp.dot(tp[j, p], _block_diag(xy[p][:L].astype(BF16), lo_half),
                            preferred_element_type=F32).astype(BF16) for p in pairs]
            yield
            ys.append(jnp.concatenate(
                [xy[p][L:] + jnp.dot(a_rb[j, p], _block_diag(u_bf[p], lo_half),
                                     preferred_element_type=F32) for p in pairs], axis=1))
            for p in pairs:
                s_full = jnp.dot(g_bkh[j, p], jnp.concatenate([u_bf[p], vp[j, p]], axis=0),
                                 preferred_element_type=F32)
                st[p] = g_dp[j, p] * st[p] + jnp.where(lo_half, s_full[:L], s_full[L:])
            yield
        for p in pairs:
            st_ref[p] = st[p]
        y = jnp.concatenate(ys, axis=0)
        mean = head_sum(y) * inv_n
        yield
        yc = y - mean
        var = head_sum(yc * yc) * inv_n
        yield
        yn = yc * lax.rsqrt(var + GN_EPS) * lng_ref[...] + lnb_ref[...]
        o_ref[...] = ((yn + g_bonus) * g_gate).astype(BF16)

    gens = {"F": front(), "B": back()}
    for token in RWKV_ORDER:
        next(gens[token], None)
    for gen in gens.values():
        for _ in gen:
            pass


def _rwkv(proj, lora, b, s, mu_rkv, w0, a0, k_k, k_a, r_k, ln_g, ln_b, w2a, ones_bd, tri):
    rows = RWKV_SUB * CHUNK
    nblk = s // rows
    shift = jnp.eye(rows, PREV_ROWS + rows, k=PREV_ROWS - 1, dtype=BF16)
    in_row = lambda bi, ti: bi * nblk + jnp.minimum(ti, nblk - 1)
    out_row = lambda bi, ti: bi * nblk + jnp.maximum(ti - 1, 0)
    col = lambda j: pl.BlockSpec((rows, D_BRANCH), lambda bi, ti: (in_row(bi, ti), j))
    const = lambda shape: pl.BlockSpec(shape, lambda bi, ti: (0,) * len(shape))
    return pl.pallas_call(
        _rwkv_kernel,
        grid=(b, nblk + 1),
        in_specs=[
            col(0), col(1), col(2), col(3),
            pl.BlockSpec((rows, 4 * LORA), lambda bi, ti: (in_row(bi, ti), 0)),
            const((3, D_BRANCH)), const((1, D_BRANCH)), const((1, D_BRANCH)),
            const((1, D_BRANCH)), const((1, D_BRANCH)), const((1, D_BRANCH)),
            const((1, D_BRANCH)), const((1, D_BRANCH)),
            const((2 * LORA, 2 * D_BRANCH)), const((LANES, LANES)), const((CHUNK, 2 * CHUNK)),
            const((rows, PREV_ROWS + rows)),
        ],
        out_specs=pl.BlockSpec((rows, D_BRANCH), lambda bi, ti: (out_row(bi, ti), 0)),
        out_shape=jax.ShapeDtypeStruct((b * s, D_BRANCH), BF16),
        scratch_shapes=[
            pltpu.VMEM((PREV_ROWS, D_BRANCH), BF16),
            pltpu.VMEM((PREV_ROWS, D_BRANCH), BF16),
            pltpu.VMEM((PREV_ROWS, D_BRANCH), BF16),
            pltpu.VMEM((PREV_ROWS, LANES), BF16),
            pltpu.VMEM((N_PAIRS, HEAD_DIM, LANES), F32),
            pltpu.VMEM((RWKV_SUB, len(STAGED), CHUNK, D_BRANCH), BF16),
            pltpu.VMEM((RWKV_SUB, LANES, D_BRANCH), BF16),
            pltpu.VMEM((RWKV_SUB, CHUNK, D_BRANCH), F32),
            pltpu.VMEM((2, rows, D_BRANCH), F32),
        ],
        compiler_params=pltpu.CompilerParams(
            dimension_semantics=("arbitrary", "arbitrary"),
            vmem_limit_bytes=VMEM_LIMIT),
        name="rwkv7",
    )(proj, proj, proj, proj, lora, mu_rkv, w0, a0, k_k, k_a, r_k, ln_g, ln_b, w2a, ones_bd, tri,
      shift)


def _attn_kernel(q_ref, k_ref, v_ref, z_ref, bias_ref, o_ref, kwin_ref, vwin_ref):
    i = pl.program_id(1)
    past = PAST_CHUNKS * CHUNK

    @pl.when(i == 0)
    def _():
        kwin_ref[...] = jnp.zeros_like(kwin_ref)
        vwin_ref[...] = jnp.zeros_like(vwin_ref)

    kwin_ref[past:ATT_K, :] = k_ref[...]
    vwin_ref[past:ATT_K, :] = v_ref[...]

    def body(mask_start):
        q = (q_ref[...].astype(F32) * (HEAD_DIM ** -0.5 * LOG2E)).astype(BF16)
        kcat = kwin_ref[...]
        vcat = vwin_ref[...]

        if mask_start:
            klane = lax.broadcasted_iota(jnp.int32, (1, ATT_K), 1)
            kchunk = ATT_QCH * i - PAST_CHUNKS + klane // CHUNK
            start_mask = jnp.where(kchunk >= 0, 0.0, NEG_INF).astype(F32)

        lane = lax.broadcasted_iota(jnp.int32, (ATT_Q, LANES), 1)
        lo_half = lane < HEAD_DIM
        nt = (((1,), (1,)), ((), ()))
        outs = []
        for g0 in range(0, N_HEADS, ATT_GROUP):
            heads = range(g0, g0 + ATT_GROUP)
            sl = {h: slice((h // 2) * LANES, (h // 2 + 1) * LANES) for h in heads}
            qm = {h: jnp.where(lo_half if h % 2 == 0 else jnp.logical_not(lo_half),
                               q[:, sl[h]], jnp.zeros((ATT_Q, LANES), BF16)) for h in heads}
            s = {h: lax.dot_general(qm[h], kcat[:, sl[h]], nt, preferred_element_type=F32)
                 + bias_ref[h] for h in heads}
            if mask_start:
                s = {h: s[h] + start_mask for h in heads}
            m = {h: jnp.max(s[h], axis=-1, keepdims=True) for h in heads}
            e = {h: jnp.exp2(s[h] - m[h]) for h in heads}
            l = {h: jnp.sum(e[h], axis=-1, keepdims=True) for h in heads}
            oh = {h: jnp.dot(e[h].astype(BF16), vcat[:, sl[h]],
                             preferred_element_type=F32) / l[h] for h in heads}
            for h in heads[::2]:
                outs.append(jnp.where(lo_half, oh[h], oh[h + 1]))
        o = jnp.concatenate(outs, axis=1)
        z = z_ref[...].astype(F32)
        o_ref[...] = (o * (z * jax.nn.sigmoid(z))).astype(BF16)
        kwin_ref[0:past, :] = kcat[ATT_Q:, :]
        vwin_ref[0:past, :] = vcat[ATT_Q:, :]

    first_blocks = PAST_CHUNKS // ATT_QCH
    pl.when(i < first_blocks)(functools.partial(body, True))
    pl.when(i >= first_blocks)(functools.partial(body, False))


def _attention(proj, b, s, bias):
    nq = s // ATT_Q
    col = lambda j: pl.BlockSpec((ATT_Q, D_BRANCH), lambda bi, qi: (bi * nq + qi, j))
    return pl.pallas_call(
        _attn_kernel,
        grid=(b, nq),
        in_specs=[col(4), col(5), col(6), col(7),
                  pl.BlockSpec((N_HEADS, ATT_Q, ATT_K), lambda bi, qi: (0, 0, 0),
                               pipeline_mode=pl.Buffered(1))],
        out_specs=col(0),
        out_shape=jax.ShapeDtypeStruct((b * s, D_BRANCH), BF16),
        scratch_shapes=[pltpu.VMEM((ATT_K, D_BRANCH), BF16),
                        pltpu.VMEM((ATT_K, D_BRANCH), BF16)],
        compiler_params=pltpu.CompilerParams(
            dimension_semantics=("arbitrary", "arbitrary"),
            vmem_limit_bytes=VMEM_LIMIT),
        name="band_attn",
    )(proj, proj, proj, proj, bias)


BIAS_W = 768


def _bias_kernel(g_ref, o_ref):
    g = jnp.broadcast_to(g_ref[0], (ATT_Q, BIAS_W))
    t = pltpu.roll(g, 0, 1, stride=1, stride_axis=0)[:, :ATT_K]
    qc = lax.broadcasted_iota(jnp.int32, (ATT_Q, ATT_K), 0) // CHUNK
    kc = lax.broadcasted_iota(jnp.int32, (ATT_Q, ATT_K), 1) // CHUNK - PAST_CHUNKS
    valid = (kc <= qc) & (kc >= qc - PAST_CHUNKS)
    o_ref[0] = jnp.where(valid, t * LOG2E, NEG_INF)


def _attn_bias_table(rel_bias):
    m = np.arange(BIAS_W)
    m = np.where(m < ATT_K, m, m - BIAS_W)
    idx = np.clip(PAST_CHUNKS * CHUNK - m, -REL_CLIP, REL_CLIP) + REL_CLIP
    g = rel_bias[:, idx].astype(F32).reshape(N_HEADS, 1, BIAS_W)
    return pl.pallas_call(
        _bias_kernel,
        grid=(N_HEADS,),
        in_specs=[pl.BlockSpec((1, 1, BIAS_W), lambda h: (h, 0, 0))],
        out_specs=pl.BlockSpec((1, ATT_Q, ATT_K), lambda h: (h, 0, 0)),
        out_shape=jax.ShapeDtypeStruct((N_HEADS, ATT_Q, ATT_K), F32),
        name="rel_bias_table",
    )(g)


def _out_kernel(x_ref, yr_ref, ya_ref, mr_ref, ma_ref, wbr_ref, wba_ref, wout_ref,
                bm_ref, pg_ref, o_ref):
    half = x_ref.shape[0] // 2
    rows = [slice(0, half), slice(half, 2 * half)]
    u_r = [jnp.dot(yr_ref[r, :], wbr_ref[...], preferred_element_type=F32) for r in rows]
    u_a = [jnp.dot(ya_ref[r, :], wba_ref[...], preferred_element_type=F32) for r in rows]
    merged = []
    for h, r in enumerate(rows):
        g_r = jax.nn.sigmoid(mr_ref[r, :].astype(F32) + bm_ref[0:1, :])
        g_a = jax.nn.sigmoid(ma_ref[r, :].astype(F32) + bm_ref[1:2, :])
        merged.append((g_r * u_r[h] + g_a * u_a[h]).astype(BF16))
    o = [jnp.dot(mg, wout_ref[...], preferred_element_type=F32) for mg in merged]
    for h, r in enumerate(rows):
        ms = jnp.mean(o[h] * o[h], axis=-1, keepdims=True)
        o_ref[r, :] = x_ref[r, :] + o[h] * lax.rsqrt(ms + NORM_EPS) * pg_ref[...]


def _outproj(x2, y_r, y_a, proj, w_br, w_ba, w_out, b_merge, post_g, tm=OUT_TM):
    m = x2.shape[0]
    const = lambda shape: pl.BlockSpec(shape, lambda i: (0,) * len(shape),
                                       pipeline_mode=pl.Buffered(1))
    return pl.pallas_call(
        _out_kernel,
        grid=(m // tm,),
        in_specs=[
            pl.BlockSpec((tm, D_MODEL), lambda i: (i, 0)),
            pl.BlockSpec((tm, D_BRANCH), lambda i: (i, 0)),
            pl.BlockSpec((tm, D_BRANCH), lambda i: (i, 0)),
            pl.BlockSpec((tm, D_MODEL), lambda i: (i, 4)),
            pl.BlockSpec((tm, D_MODEL), lambda i: (i, 5)),
            const((D_BRANCH, D_MODEL)), const((D_BRANCH, D_MODEL)),
            const((D_MODEL, D_MODEL)), const((2, D_MODEL)), const((1, D_MODEL)),
        ],
        out_specs=pl.BlockSpec((tm, D_MODEL), lambda i: (i, 0)),
        out_shape=jax.ShapeDtypeStruct((m, D_MODEL), F32),
        compiler_params=pltpu.CompilerParams(
            dimension_semantics=("parallel",),
            vmem_limit_bytes=VMEM_LIMIT),
        name="merge_outproj",
    )(x2, y_r, y_a, proj, proj, w_br, w_ba, w_out, b_merge, post_g)


def _layer(x2, b, s, pre_g, post_g, w_in, mu_rkv, mu_wa, w0, w1, w2, a0, a1, a2,
           k_k, k_a, r_k, ln_g, ln_b, rel_bias, w_br, w_ba, b_merge, w_out):
    mw, ma = mu_wa[0][:, None], mu_wa[1][:, None]
    w_lora = jnp.concatenate(
        [(1.0 - mw) * w1, (1.0 - ma) * a1, mw * w1, ma * a1], axis=1).astype(BF16)
    zeros = jnp.zeros((LORA, D_BRANCH), F32)
    w2a = jnp.concatenate([jnp.concatenate([w2, zeros], axis=1),
                           jnp.concatenate([zeros, a2], axis=1)], axis=0).astype(BF16)
    lane = jnp.arange(LANES)
    ones_bd = (lane[:, None] // HEAD_DIM == lane[None, :] // HEAD_DIM).astype(BF16)
    tri = jnp.tril(jnp.ones((CHUNK, CHUNK), BF16))
    tri = jnp.concatenate([tri, tri], axis=1)
    row = lambda t: t.reshape(1, -1).astype(F32)

    proj, lora = _inproj(x2, row(pre_g), w_in.astype(BF16), w_lora)
    y_r = _rwkv(proj, lora, b, s, mu_rkv.astype(F32), row(w0), row(a0), row(k_k), row(k_a),
                row(r_k), row(ln_g), row(ln_b), w2a, ones_bd, tri)
    y_a = _attention(proj, b, s, _attn_bias_table(rel_bias))
    return _outproj(x2, y_r, y_a, proj, w_br.astype(BF16), w_ba.astype(BF16),
                    w_out.astype(BF16), b_merge.astype(F32), row(post_g))


def kernel(x, pre_norm_g, post_norm_g, w_in, mu_rkv, mu_wa, w0, w1, w2, a0, a1, a2,
           k_k, k_a, r_k, ln_x_g, ln_x_b, rel_bias, w_branch_rwkv, w_branch_attn,
           b_merge, w_out):
    b, s, d = x.shape
    assert d == D_MODEL and s % ATT_Q == 0 and (b * s) % INPROJ_TM == 0
    x2 = x.reshape(b * s, d)
    for l in range(pre_norm_g.shape[0]):
        x2 = _layer(x2, b, s, pre_norm_g[l], post_norm_g[l], w_in[l], mu_rkv[l], mu_wa[l],
                    w0[l], w1[l], w2[l], a0[l], a1[l], a2[l], k_k[l], k_a[l], r_k[l],
                    ln_x_g[l], ln_x_b[l], rel_bias[l], w_branch_rwkv[l], w_branch_attn[l],
                    b_merge[l], w_out[l])
    return x2.reshape(b, s, d)
```

```python
import functools

import jax
import jax.numpy as jnp
import numpy as np
from jax import lax
from jax.experimental import pallas as pl
from jax.experimental.pallas import tpu as pltpu

F32 = jnp.float32
BF16 = jnp.bfloat16

D_MODEL = 2048
D_BRANCH = 1024
HEAD_DIM = 64
N_HEADS = 16
N_PAIRS = N_HEADS // 2
LANES = 128
CHUNK = 64
LORA = 64
DECAY_SCALE = 0.606531
PAST_CHUNKS = 8
REL_CLIP = 256
NORM_EPS = 1e-6
GN_EPS = 64e-5
NEG_INF = -1e30
LOG2E = 1.4426950408889634
D_IN = 8 * D_BRANCH + 2 * D_MODEL
VMEM_LIMIT = 56 * 1024 * 1024
INPROJ_TM = 1024
INPROJ_TN = 2048
INPROJ_CG = 256
INPROJ_VMEM_LIMIT = 60 * 1024 * 1024
OUT_TM = 512
RWKV_SUB = 2
STAGED = ("at", "rt", "vp", "a_ak", "a_rb", "a_rk", "tp")
RWKV_ORDER = "FB" * 9

ATT_QCH = 2
ATT_Q = ATT_QCH * CHUNK
ATT_KCH = PAST_CHUNKS + ATT_QCH
ATT_K = ATT_KCH * CHUNK
ATT_GROUP = 4


def _shift_rows(x, carry):
    rolled = pltpu.roll(x, 1, 0)
    first = lax.broadcasted_iota(jnp.int32, (8, x.shape[1]), 0) == 0
    return jnp.concatenate([jnp.where(first, carry, rolled[:8]), rolled[8:]], axis=0)


def _inproj_kernel(x_ref, g_ref, w_ref, wl_ref, w2a_ref, mu_ref, w0_ref, a0_ref,
                   o_ref, lw_ref, a_ref, hn_ref, carry_ref, carry_lo_ref, *, blocks_per_seq):
    i, j = pl.program_id(0), pl.program_id(1)
    tm, tn = o_ref.shape
    half = tn // 2
    keep = jnp.where(i % blocks_per_seq == 0, 0.0, 1.0)

    @pl.when((i == 0) & (j == 0))
    def _():
        carry_ref[...] = jnp.zeros_like(carry_ref)
        carry_lo_ref[...] = jnp.zeros_like(carry_lo_ref)

    def cols(c0):
        return jnp.dot(hn_ref[...], w_ref[:, c0:c0 + INPROJ_CG], preferred_element_type=F32)

    def lerp(acc, c0, carry_c0, mu_row):
        cs = slice(carry_c0, carry_c0 + INPROJ_CG)
        prev = _shift_rows(acc, carry_ref[0:1, cs] * keep)
        carry_ref[0:1, cs] = acc[tm - 1:tm, :]
        mu = mu_ref[mu_row:mu_row + 1, c0 % half:c0 % half + INPROJ_CG]
        return acc + (prev - acc) * mu

    def silu(acc):
        return acc * jax.nn.sigmoid(acc)

    def emit(epilogue, order=range(0, tn, INPROJ_CG), between=None):
        for c0 in order:
            o_ref[:, c0:c0 + INPROJ_CG] = epilogue(cols(c0), c0).astype(BF16)
            if between is not None:
                next(between, None)

    def low_rank(hid):
        for c0 in range(0, D_BRANCH, INPROJ_CG):
            cs = slice(c0, c0 + INPROJ_CG)
            dec_w = jnp.dot(hid, w2a_ref[:, cs], preferred_element_type=F32)
            lw_ref[:, cs] = -DECAY_SCALE * jax.nn.sigmoid(w0_ref[:, cs] + dec_w)
            yield
            dec_a = jnp.dot(hid, w2a_ref[:, D_BRANCH + c0:D_BRANCH + c0 + INPROJ_CG],
                            preferred_element_type=F32)
            a_ref[:, cs] = jax.nn.sigmoid(a0_ref[:, cs] + dec_a).astype(BF16)
            yield

    @pl.when(j == 0)
    def _():
        x = x_ref[...]
        ms = jnp.mean(x * x, axis=-1, keepdims=True)
        hn_ref[...] = (x * lax.rsqrt(ms + NORM_EPS) * g_ref[...]).astype(BF16)
        lo = jnp.dot(hn_ref[...], wl_ref[...], preferred_element_type=F32)
        hid = lo[:, :LANES] + _shift_rows(lo[:, LANES:], carry_lo_ref[0:1, :] * keep)
        carry_lo_ref[0:1, :] = lo[tm - 1:tm, LANES:]
        hid = jnp.where(lax.broadcasted_iota(jnp.int32, (tm, LANES), 1) < LORA,
                        jnp.tanh(hid), hid).astype(BF16)
        emit(lambda acc, c0: lerp(acc, c0, c0, c0 // half), between=low_rank(hid))

    @pl.when(j == 1)
    def _():
        emit(lambda acc, c0: lerp(acc, c0, tn + c0, 2) if c0 < half else silu(acc))

    @pl.when(j == 3)
    def _():
        emit(lambda acc, c0: acc if c0 < half else silu(acc),
             order=list(range(half, tn, INPROJ_CG)) + list(range(0, half, INPROJ_CG)))

    @pl.when((j == 2) | (j >= 4))
    def _():
        emit(lambda acc, c0: acc)


def _inproj(x2, g, w_in, w_lora, w2a, mu_rkv, w0, a0, s, tm=INPROJ_TM, tn=INPROJ_TN):
    m = x2.shape[0]
    n = w_in.shape[1]
    assert tn == 2 * D_BRANCH and s % tm == 0
    const = lambda shape: pl.BlockSpec(shape, lambda i, j: (0,) * len(shape))
    return pl.pallas_call(
        functools.partial(_inproj_kernel, blocks_per_seq=s // tm),
        grid=(m // tm, n // tn),
        in_specs=[
            pl.BlockSpec((tm, D_MODEL), lambda i, j: (i, 0)),
            const((1, D_MODEL)),
            pl.BlockSpec((D_MODEL, tn), lambda i, j: (0, j)),
            const((D_MODEL, 4 * LORA)), const((2 * LORA, 2 * D_BRANCH)),
            const((3, D_BRANCH)), const((1, D_BRANCH)), const((1, D_BRANCH)),
        ],
        out_specs=[pl.BlockSpec((tm, tn), lambda i, j: (i, j)),
                   pl.BlockSpec((tm, D_BRANCH), lambda i, j: (i, 0)),
                   pl.BlockSpec((tm, D_BRANCH), lambda i, j: (i, 0))],
        out_shape=[jax.ShapeDtypeStruct((m, n), BF16),
                   jax.ShapeDtypeStruct((m, D_BRANCH), F32),
                   jax.ShapeDtypeStruct((m, D_BRANCH), BF16)],
        scratch_shapes=[pltpu.VMEM((tm, D_MODEL), BF16),
                        pltpu.VMEM((8, 3 * D_BRANCH), F32),
                        pltpu.VMEM((8, LANES), F32)],
        compiler_params=pltpu.CompilerParams(
            dimension_semantics=("arbitrary", "arbitrary"),
            vmem_limit_bytes=INPROJ_VMEM_LIMIT),
        name="inproj",
    )(x2, g, w_in, w_lora, w2a, mu_rkv, w0, a0)


def _block_diag(x, lo_half):
    zero = jnp.zeros_like(x)
    return jnp.concatenate(
        [jnp.where(lo_half, x, zero), jnp.where(lo_half, zero, x)], axis=0)


def _rwkv_kernel(r_ref, k_ref, v_ref, gate_ref, lw_ref, a_ref,
                 kk_ref, ka_ref, rk_ref, lng_ref, lnb_ref, ones_ref, tri_ref,
                 o_ref,
                 st_ref, s16_ref, sbk_ref, sdp_ref, sfin_ref):
    t = pl.program_id(1)
    L = CHUNK
    R = r_ref.shape[0]
    subs = range(R // L)
    pairs = range(N_PAIRS)
    sls = [slice(p * LANES, (p + 1) * LANES) for p in pairs]
    units = [(j, p) for j in subs for p in pairs]

    @pl.when(t == 0)
    def _():
        for ref in (st_ref, s16_ref, sbk_ref, sdp_ref, sfin_ref):
            ref[...] = jnp.zeros_like(ref)

    ones_bd = ones_ref[...]
    lane1 = lax.broadcasted_iota(jnp.int32, (L, LANES), 1)
    row = lax.broadcasted_iota(jnp.int32, (L, LANES), 0)
    li = lane1 & (HEAD_DIM - 1)
    strict = row > li
    incl = row >= li
    lo_half = lane1 < HEAD_DIM
    eye_p = (row == li).astype(F32)
    nt = (((1,), (1,)), ((), ()))
    inv_n = 1.0 / HEAD_DIM

    def head_sum(x):
        n = x.shape[0]
        xs = jnp.concatenate([x[:, sl] for sl in sls], axis=0).astype(BF16)
        s = jnp.dot(xs, ones_bd, preferred_element_type=F32)
        return jnp.concatenate([s[p * n:(p + 1) * n] for p in pairs], axis=1)

    staged = {name: {(j, p): s16_ref[j, i, :, sls[p]] for (j, p) in units}
              for i, name in enumerate(STAGED)}
    g_bkh = {(j, p): sbk_ref[j, :, sls[p]] for (j, p) in units}
    g_dp = {(j, p): sdp_ref[j, :, sls[p]] for (j, p) in units}
    g_bonus, g_gate = sfin_ref[0], sfin_ref[1]

    def front():
        r, k, v_bf = r_ref[...].astype(F32), k_ref[...].astype(F32), v_ref[...]
        lw, a = lw_ref[...], a_ref[...].astype(F32)
        kk = k * kk_ref[...]
        kk_ss = head_sum(kk * kk)
        k = k * (1.0 + (a - 1.0) * ka_ref[...])
        rk_sum = head_sum(r * k * rk_ref[...])
        yield
        kk = kk * jnp.minimum(lax.rsqrt(kk_ss), 1e12)
        ap = -kk
        bp = kk * a
        sfin_ref[0] = rk_sum * v_bf.astype(F32)
        sfin_ref[1] = gate_ref[...].astype(F32)
        lw_hi = lw.astype(BF16)
        lw_lo = (lw - lw_hi.astype(F32)).astype(BF16)
        cums = [jnp.dot(tri_ref[...],
                        jnp.concatenate([lw_hi[j * L:(j + 1) * L], lw_lo[j * L:(j + 1) * L]],
                                        axis=0),
                        preferred_element_type=F32) for j in subs]
        yield
        nm, a_ak, a_rb, a_rk = {}, {}, {}, {}
        for j in subs:
            rs = slice(j * L, (j + 1) * L)
            cum = cums[j]
            e_pos = jnp.exp(cum)
            e_neg = 1.0 / e_pos
            p_l = e_pos[L - 1:L, :]
            e_end = p_l * e_neg
            at_all = (ap[rs] * jnp.exp(cum - lw[rs])).astype(BF16)
            rt_all = (r[rs] * e_pos).astype(BF16)
            bt_all = (bp[rs] * e_neg).astype(BF16)
            kt_all = (k[rs] * e_neg).astype(BF16)
            bh_all = bp[rs] * e_end
            kh_all = k[rs] * e_end
            s16_ref[j, STAGED.index("at")] = at_all
            s16_ref[j, STAGED.index("rt")] = rt_all
            s16_ref[j, STAGED.index("vp")] = v_bf[rs]
            bkh_t, dps = [], []
            for p in pairs:
                bkh = jnp.concatenate([bh_all[:, sls[p]], kh_all[:, sls[p]]], axis=0)
                bkh_t.append(bkh.T.astype(BF16))
                dcol = jnp.broadcast_to(p_l[:, sls[p]], (LANES, LANES)).T
                dps.append(jnp.where(lo_half, dcol[:L], dcol[L:]))
                aa = lax.dot_general(
                    jnp.concatenate([at_all[:, sls[p]], rt_all[:, sls[p]]], axis=0),
                    jnp.concatenate([_block_diag(bt_all[:, sls[p]], lo_half),
                                     _block_diag(kt_all[:, sls[p]], lo_half)], axis=0),
                    nt, preferred_element_type=F32)
                nm[j, p] = jnp.where(strict, aa[:L, :LANES], 0.0)
                a_ak[j, p] = jnp.where(strict, aa[:L, LANES:], 0.0).astype(BF16)
                a_rb[j, p] = jnp.where(incl, aa[L:, :LANES], 0.0).astype(BF16)
                a_rk[j, p] = jnp.where(incl, aa[L:, LANES:], 0.0).astype(BF16)
            sbk_ref[j] = jnp.concatenate(bkh_t, axis=1)
            sdp_ref[j] = jnp.concatenate(dps, axis=1)
            for name, d in (("a_ak", a_ak), ("a_rb", a_rb), ("a_rk", a_rk)):
                s16_ref[j, STAGED.index(name)] = jnp.concatenate([d[j, p] for p in pairs], axis=1)
        yield
        tp = {u: eye_p + nm[u] for u in units}
        nb = {u: nm[u].astype(BF16) for u in units}
        nm = {u: jnp.dot(nb[u], _block_diag(nb[u], lo_half), preferred_element_type=F32)
              for u in units}
        yield
        for _ in range(4):
            nb = {u: nm[u].astype(BF16) for u in units}
            tn = {u: jnp.dot(jnp.concatenate([tp[u].astype(BF16), nb[u]], axis=0),
                             _block_diag(nb[u], lo_half), preferred_element_type=F32)
                  for u in units}
            tp = {u: tp[u] + tn[u][:L] for u in units}
            nm = {u: tn[u][L:] for u in units}
            yield
        tp = {u: (tp[u] + jnp.dot(tp[u].astype(BF16), _block_diag(nm[u].astype(BF16), lo_half),
                                  preferred_element_type=F32)).astype(BF16) for u in units}
        for j in subs:
            s16_ref[j, STAGED.index("tp")] = jnp.concatenate([tp[j, p] for p in pairs], axis=1)

    def back():
        at, rt, vp = staged["at"], staged["rt"], staged["vp"]
        a_ak, a_rb, a_rk, tp = staged["a_ak"], staged["a_rb"], staged["a_rk"], staged["tp"]
        st = [st_ref[p] for p in pairs]
        ys = []
        for j in subs:
            sv = [jnp.concatenate([_block_diag(st[p].astype(BF16), lo_half),
                                   _block_diag(vp[j, p], lo_half)], axis=0) for p in pairs]
            xy = [jnp.dot(jnp.concatenate([jnp.concatenate([at[j, p], a_ak[j, p]], axis=1),
                                           jnp.concatenate([rt[j, p], a_rk[j, p]], axis=1)],
                                          axis=0),
                          sv[p], preferred_element_type=F32) for p in pairs]
            yield
            u_bf = [jnp.dot(tp[j, p], _block_diag(xy[p][:L].astype(BF16), lo_half),
                            preferred_element_type=F32).astype(BF16) for p in pairs]
            yield
            ys.append(jnp.concatenate(
                [xy[p][L:] + jnp.dot(a_rb[j, p], _block_diag(u_bf[p], lo_half),
                                     preferred_element_type=F32) for p in pairs], axis=1))
            for p in pairs:
                s_full = jnp.dot(g_bkh[j, p], jnp.concatenate([u_bf[p], vp[j, p]], axis=0),
                                 preferred_element_type=F32)
                st[p] = g_dp[j, p] * st[p] + jnp.where(lo_half, s_full[:L], s_full[L:])
            yield
        for p in pairs:
            st_ref[p] = st[p]
        y = jnp.concatenate(ys, axis=0)
        mean = head_sum(y) * inv_n
        yield
        yc = y - mean
        var = head_sum(yc * yc) * inv_n
        yield
        yn = yc * lax.rsqrt(var + GN_EPS) * lng_ref[...] + lnb_ref[...]
        o_ref[...] = ((yn + g_bonus) * g_gate).astype(BF16)

    gens = {"F": front(), "B": back()}
    for token in RWKV_ORDER:
        next(gens[token], None)
    for gen in gens.values():
        for _ in gen:
            pass


def _rwkv(proj, lw, a, b, s, k_k, k_a, r_k, ln_g, ln_b, ones_bd, tri):
    rows = RWKV_SUB * CHUNK
    nblk = s // rows
    in_row = lambda bi, ti: bi * nblk + jnp.minimum(ti, nblk - 1)
    out_row = lambda bi, ti: bi * nblk + jnp.maximum(ti - 1, 0)
    col = lambda j: pl.BlockSpec((rows, D_BRANCH), lambda bi, ti: (in_row(bi, ti), j))
    const = lambda shape: pl.BlockSpec(shape, lambda bi, ti: (0,) * len(shape))
    return pl.pallas_call(
        _rwkv_kernel,
        grid=(b, nblk + 1),
        in_specs=[
            col(0), col(1), col(2), col(3), col(0), col(0),
            const((1, D_BRANCH)), const((1, D_BRANCH)), const((1, D_BRANCH)),
            const((1, D_BRANCH)), const((1, D_BRANCH)),
            const((LANES, LANES)), const((CHUNK, 2 * CHUNK)),
        ],
        out_specs=pl.BlockSpec((rows, D_BRANCH), lambda bi, ti: (out_row(bi, ti), 0)),
        out_shape=jax.ShapeDtypeStruct((b * s, D_BRANCH), BF16),
        scratch_shapes=[
            pltpu.VMEM((N_PAIRS, HEAD_DIM, LANES), F32),
            pltpu.VMEM((RWKV_SUB, len(STAGED), CHUNK, D_BRANCH), BF16),
            pltpu.VMEM((RWKV_SUB, LANES, D_BRANCH), BF16),
            pltpu.VMEM((RWKV_SUB, CHUNK, D_BRANCH), F32),
            pltpu.VMEM((2, rows, D_BRANCH), F32),
        ],
        compiler_params=pltpu.CompilerParams(
            dimension_semantics=("arbitrary", "arbitrary"),
            vmem_limit_bytes=VMEM_LIMIT),
        name="rwkv7",
    )(proj, proj, proj, proj, lw, a, k_k, k_a, r_k, ln_g, ln_b, ones_bd, tri)


def _attn_kernel(q_ref, k_ref, v_ref, gate_ref, bias_ref, o_ref, kwin_ref, vwin_ref):
    i = pl.program_id(1)
    past = PAST_CHUNKS * CHUNK

    @pl.when(i == 0)
    def _():
        kwin_ref[...] = jnp.zeros_like(kwin_ref)
        vwin_ref[...] = jnp.zeros_like(vwin_ref)

    kwin_ref[past:ATT_K, :] = k_ref[...]
    vwin_ref[past:ATT_K, :] = v_ref[...]

    def body(mask_start):
        q = (q_ref[...].astype(F32) * (HEAD_DIM ** -0.5 * LOG2E)).astype(BF16)
        kcat = kwin_ref[...]
        vcat = vwin_ref[...]

        if mask_start:
            klane = lax.broadcasted_iota(jnp.int32, (1, ATT_K), 1)
            kchunk = ATT_QCH * i - PAST_CHUNKS + klane // CHUNK
            start_mask = jnp.where(kchunk >= 0, 0.0, NEG_INF).astype(F32)

        lane = lax.broadcasted_iota(jnp.int32, (ATT_Q, LANES), 1)
        lo_half = lane < HEAD_DIM
        nt = (((1,), (1,)), ((), ()))
        outs = []
        for g0 in range(0, N_HEADS, ATT_GROUP):
            heads = range(g0, g0 + ATT_GROUP)
            sl = {h: slice((h // 2) * LANES, (h // 2 + 1) * LANES) for h in heads}
            qm = {h: jnp.where(lo_half if h % 2 == 0 else jnp.logical_not(lo_half),
                               q[:, sl[h]], jnp.zeros((ATT_Q, LANES), BF16)) for h in heads}
            s = {h: lax.dot_general(qm[h], kcat[:, sl[h]], nt, preferred_element_type=F32)
                 + bias_ref[h] for h in heads}
            if mask_start:
                s = {h: s[h] + start_mask for h in heads}
            m = {h: jnp.max(s[h], axis=-1, keepdims=True) for h in heads}
            e = {h: jnp.exp2(s[h] - m[h]) for h in heads}
            l = {h: jnp.sum(e[h], axis=-1, keepdims=True) for h in heads}
            oh = {h: jnp.dot(e[h].astype(BF16), vcat[:, sl[h]],
                             preferred_element_type=F32) / l[h] for h in heads}
            for h in heads[::2]:
                outs.append(jnp.where(lo_half, oh[h], oh[h + 1]))
        o = jnp.concatenate(outs, axis=1)
        o_ref[...] = (o * gate_ref[...].astype(F32)).astype(BF16)
        kwin_ref[0:past, :] = kcat[ATT_Q:, :]
        vwin_ref[0:past, :] = vcat[ATT_Q:, :]

    first_blocks = PAST_CHUNKS // ATT_QCH
    pl.when(i < first_blocks)(functools.partial(body, True))
    pl.when(i >= first_blocks)(functools.partial(body, False))


def _attention(proj, b, s, bias):
    nq = s // ATT_Q
    col = lambda j: pl.BlockSpec((ATT_Q, D_BRANCH), lambda bi, qi: (bi * nq + qi, j))
    return pl.pallas_call(
        _attn_kernel,
        grid=(b, nq),
        in_specs=[col(4), col(5), col(6), col(7),
                  pl.BlockSpec((N_HEADS, ATT_Q, ATT_K), lambda bi, qi: (0, 0, 0),
                               pipeline_mode=pl.Buffered(1))],
        out_specs=col(0),
        out_shape=jax.ShapeDtypeStruct((b * s, D_BRANCH), BF16),
        scratch_shapes=[pltpu.VMEM((ATT_K, D_BRANCH), BF16),
                        pltpu.VMEM((ATT_K, D_BRANCH), BF16)],
        compiler_params=pltpu.CompilerParams(
            dimension_semantics=("arbitrary", "arbitrary"),
            vmem_limit_bytes=VMEM_LIMIT),
        name="band_attn",
    )(proj, proj, proj, proj, bias)


BIAS_W = 768


def _bias_kernel(g_ref, o_ref):
    g = jnp.broadcast_to(g_ref[0], (ATT_Q, BIAS_W))
    t = pltpu.roll(g, 0, 1, stride=1, stride_axis=0)[:, :ATT_K]
    qc = lax.broadcasted_iota(jnp.int32, (ATT_Q, ATT_K), 0) // CHUNK
    kc = lax.broadcasted_iota(jnp.int32, (ATT_Q, ATT_K), 1) // CHUNK - PAST_CHUNKS
    valid = (kc <= qc) & (kc >= qc - PAST_CHUNKS)
    o_ref[0] = jnp.where(valid, t * LOG2E, NEG_INF)


def _attn_bias_table(rel_bias):
    m = np.arange(BIAS_W)
    m = np.where(m < ATT_K, m, m - BIAS_W)
    idx = np.clip(PAST_CHUNKS * CHUNK - m, -REL_CLIP, REL_CLIP) + REL_CLIP
    g = rel_bias[:, idx].astype(F32).reshape(N_HEADS, 1, BIAS_W)
    return pl.pallas_call(
        _bias_kernel,
        grid=(N_HEADS,),
        in_specs=[pl.BlockSpec((1, 1, BIAS_W), lambda h: (h, 0, 0))],
        out_specs=pl.BlockSpec((1, ATT_Q, ATT_K), lambda h: (h, 0, 0)),
        out_shape=jax.ShapeDtypeStruct((N_HEADS, ATT_Q, ATT_K), F32),
        name="rel_bias_table",
    )(g)


def _out_kernel(x_ref, yr_ref, ya_ref, mr_ref, ma_ref, wbr_ref, wba_ref, wout_ref,
                bm_ref, pg_ref, o_ref):
    half = x_ref.shape[0] // 2
    rows = [slice(0, half), slice(half, 2 * half)]
    u_r = [jnp.dot(yr_ref[r, :], wbr_ref[...], preferred_element_type=F32) for r in rows]
    u_a = [jnp.dot(ya_ref[r, :], wba_ref[...], preferred_element_type=F32) for r in rows]
    merged = []
    for h, r in enumerate(rows):
        g_r = jax.nn.sigmoid(mr_ref[r, :].astype(F32) + bm_ref[0:1, :])
        g_a = jax.nn.sigmoid(ma_ref[r, :].astype(F32) + bm_ref[1:2, :])
        merged.append((g_r * u_r[h] + g_a * u_a[h]).astype(BF16))
    o = [jnp.dot(mg, wout_ref[...], preferred_element_type=F32) for mg in merged]
    for h, r in enumerate(rows):
        ms = jnp.mean(o[h] * o[h], axis=-1, keepdims=True)
        o_ref[r, :] = x_ref[r, :] + o[h] * lax.rsqrt(ms + NORM_EPS) * pg_ref[...]


def _outproj(x2, y_r, y_a, proj, w_br, w_ba, w_out, b_merge, post_g, tm=OUT_TM):
    m = x2.shape[0]
    const = lambda shape: pl.BlockSpec(shape, lambda i: (0,) * len(shape),
                                       pipeline_mode=pl.Buffered(1))
    return pl.pallas_call(
        _out_kernel,
        grid=(m // tm,),
        in_specs=[
            pl.BlockSpec((tm, D_MODEL), lambda i: (i, 0)),
            pl.BlockSpec((tm, D_BRANCH), lambda i: (i, 0)),
            pl.BlockSpec((tm, D_BRANCH), lambda i: (i, 0)),
            pl.BlockSpec((tm, D_MODEL), lambda i: (i, 4)),
            pl.BlockSpec((tm, D_MODEL), lambda i: (i, 5)),
            const((D_BRANCH, D_MODEL)), const((D_BRANCH, D_MODEL)),
            const((D_MODEL, D_MODEL)), const((2, D_MODEL)), const((1, D_MODEL)),
        ],
        out_specs=pl.BlockSpec((tm, D_MODEL), lambda i: (i, 0)),
        out_shape=jax.ShapeDtypeStruct((m, D_MODEL), F32),
        compiler_params=pltpu.CompilerParams(
            dimension_semantics=("parallel",),
            vmem_limit_bytes=VMEM_LIMIT),
        name="merge_outproj",
    )(x2, y_r, y_a, proj, proj, w_br, w_ba, w_out, b_merge, post_g)


def _layer(x2, b, s, pre_g, post_g, w_in, mu_rkv, mu_wa, w0, w1, w2, a0, a1, a2,
           k_k, k_a, r_k, ln_g, ln_b, rel_bias, w_br, w_ba, b_merge, w_out):
    mw, ma = mu_wa[0][:, None], mu_wa[1][:, None]
    w_lora = jnp.concatenate(
        [(1.0 - mw) * w1, (1.0 - ma) * a1, mw * w1, ma * a1], axis=1).astype(BF16)
    zeros = jnp.zeros((LORA, D_BRANCH), F32)
    w2a = jnp.concatenate([jnp.concatenate([w2, zeros], axis=1),
                           jnp.concatenate([zeros, a2], axis=1)], axis=0).astype(BF16)
    lane = jnp.arange(LANES)
    ones_bd = (lane[:, None] // HEAD_DIM == lane[None, :] // HEAD_DIM).astype(BF16)
    tri = jnp.tril(jnp.ones((CHUNK, CHUNK), BF16))
    tri = jnp.concatenate([tri, tri], axis=1)
    row = lambda t: t.reshape(1, -1).astype(F32)

    proj, lw, a = _inproj(x2, row(pre_g), w_in.astype(BF16), w_lora, w2a, mu_rkv.astype(F32),
                          row(w0), row(a0), s)
    y_r = _rwkv(proj, lw, a, b, s, row(k_k), row(k_a), row(r_k), row(ln_g), row(ln_b),
                ones_bd, tri)
    y_a = _attention(proj, b, s, _attn_bias_table(rel_bias))
    return _outproj(x2, y_r, y_a, proj, w_br.astype(BF16), w_ba.astype(BF16),
                    w_out.astype(BF16), b_merge.astype(F32), row(post_g))


def kernel(x, pre_norm_g, post_norm_g, w_in, mu_rkv, mu_wa, w0, w1, w2, a0, a1, a2,
           k_k, k_a, r_k, ln_x_g, ln_x_b, rel_bias, w_branch_rwkv, w_branch_attn,
           b_merge, w_out):
    b, s, d = x.shape
    assert d == D_MODEL and s % ATT_Q == 0 and (b * s) % INPROJ_TM == 0
    x2 = x.reshape(b * s, d)
    for l in range(pre_norm_g.shape[0]):
        x2 = _layer(x2, b, s, pre_norm_g[l], post_norm_g[l], w_in[l], mu_rkv[l], mu_wa[l],
                    w0[l], w1[l], w2[l], a0[l], a1[l], a2[l], k_k[l], k_a[l], r_k[l],
                    ln_x_g[l], ln_x_b[l], rel_bias[l], w_branch_rwkv[l], w_branch_attn[l],
                    b_merge[l], w_out[l])
    return x2.reshape(b, s, d)
```
